```python
import math
import numpy as np
import jax
import jax.numpy as jnp
from jax import lax

D_MODEL = 2048
BATCH = 4
SEQ = 4096
DEPTH = 1
DEC_BATCH = 4
DEC_SEQ = 2048
PAST_LEN = 128

A_HEADS = 8
A_HEAD_DIM = 128
A_WIDTH = A_HEADS * A_HEAD_DIM
A_CHUNK = 64
B_HEADS = 8
B_QK_DIM = 64
B_V_DIM = 2 * B_QK_DIM
B_WIDTH = B_HEADS * B_V_DIM
Q_BLOCK = 128
P_HEADS = 8
P_NKEYS = 128
P_EXPERTS = P_NKEYS * P_NKEYS
P_QDIM = 256
P_HALF = P_QDIM // 2
P_TOPK = 16
P_BLOCK = 128
EPS = 1e-6
IN_SIZES = (A_WIDTH, A_WIDTH, A_WIDTH, A_WIDTH, A_WIDTH, 2 * B_HEADS * B_QK_DIM, 2 * B_HEADS * B_QK_DIM, B_WIDTH, 2 * D_MODEL)
IN_COLS = 5 * A_WIDTH + 4 * B_HEADS * B_QK_DIM + B_WIDTH + 2 * D_MODEL

kernel_name = "hybrid_hgrn2_diffattn_peer_encoder"


def rmsnorm(x, g):
    xf = x.astype(jnp.float32)
    y = xf * lax.rsqrt(jnp.mean(xf * xf, axis=-1, keepdims=True) + EPS)
    return (y * g.astype(jnp.float32)).astype(x.dtype)


def _split(z, sizes):
    idx = np.cumsum(np.array(sizes))[:-1].tolist()
    return jnp.split(z, idx, axis=-1)


def _hgrn2_scan(q, k, v, logf):
    bsz, t, h, dk = q.shape
    dv = v.shape[-1]
    n = t // A_CHUNK

    def chunks(a):
        return a.reshape(bsz, n, A_CHUNK, h, a.shape[-1]).transpose(1, 0, 3, 2, 4)

    qc, kc, vc = chunks(q), chunks(k), chunks(v)
    bc = jnp.cumsum(chunks(logf), axis=3)
    mask = jnp.tril(jnp.ones((A_CHUNK, A_CHUNK), dtype=bool))[:, :, None]

    def step(state, inp):
        qi, ki, vi, bi = inp
        o_inter = jnp.einsum('bhtd,bhde->bhte', qi * jnp.exp(bi), state)
        diff = bi[:, :, :, None, :] - bi[:, :, None, :, :]
        decay = jnp.exp(jnp.where(mask, diff, -jnp.inf))
        scores = jnp.einsum('bhtsd,bhsd->bhts', qi[:, :, :, None, :] * decay, ki)
        o_intra = jnp.einsum('bhts,bhse->bhte', scores, vi)
        b_last = bi[:, :, -1:, :]
        state = jnp.exp(b_last[:, :, 0, :])[..., None] * state + jnp.einsum('bhsd,bhse->bhde', ki * jnp.exp(b_last - bi), vi)
        return state, o_inter + o_intra

    s0 = jnp.zeros((bsz, h, dk, dv), q.dtype)
    _, o = lax.scan(step, s0, (qc, kc, vc, bc))
    return o.transpose(1, 0, 3, 2, 4).reshape(bsz, t, h, dv)


def hgrn2_mixer(q, i, zf, zb, g, lb, a_norm_g):
    bsz, t, _ = q.shape
    f32 = jnp.float32

    def heads(a):
        return a.astype(f32).reshape(bsz, t, A_HEADS, A_HEAD_DIM)

    qh = jax.nn.silu(heads(q))
    vh = heads(i)
    lbh = lb.astype(f32).reshape(2, A_HEADS, A_HEAD_DIM)

    def forget(z, lbd):
        f = lbd + (1.0 - lbd) * jax.nn.sigmoid(z)
        return 1.0 - f, jnp.log(f)

    kf, gf = forget(heads(zf), lbh[0])
    kb, gb = forget(heads(zb), lbh[1])
    o_f = _hgrn2_scan(qh, kf, vh, gf)
    o_b = jnp.flip(_hgrn2_scan(jnp.flip(qh, 1), jnp.flip(kb, 1), jnp.flip(vh, 1), jnp.flip(gb, 1)), 1)
    o = rmsnorm(o_f + o_b, a_norm_g.reshape(A_HEADS, A_HEAD_DIM)) * jax.nn.silu(heads(g))
    return o.reshape(bsz, t, A_WIDTH)


def diff_attention(q, k, v, lam, lam_init, b_norm_g):
    bsz, t, _ = q.shape
    f32 = jnp.float32
    q = q.reshape(bsz, t, B_HEADS, 2, B_QK_DIM)
    k = k.reshape(bsz, t, B_HEADS, 2, B_QK_DIM)
    v = v.reshape(bsz, t, B_HEADS, B_V_DIM)
    lf = lam.astype(f32)
    lam_full = jnp.exp(jnp.sum(lf[0] * lf[1])) - jnp.exp(jnp.sum(lf[2] * lf[3])) + lam_init
    slopes = 2.0 ** (-8.0 * jnp.arange(1, B_HEADS + 1, dtype=f32) / B_HEADS)
    nblk = t // Q_BLOCK
    qb = q.reshape(bsz, nblk, Q_BLOCK, B_HEADS, 2, B_QK_DIM).transpose(1, 0, 2, 3, 4, 5)
    scale = B_QK_DIM ** -0.5
    kpos = jnp.arange(t)

    def block(args):
        bi, qi = args
        s = jnp.einsum('bqhcd,bkhcd->bhcqk', qi, k, preferred_element_type=f32) * scale
        qpos = bi * Q_BLOCK + jnp.arange(Q_BLOCK)
        dist = jnp.abs(qpos[:, None] - kpos[None, :]).astype(f32)
        s = s - slopes[None, :, None, None, None] * dist[None, None, None]
        p = jax.nn.softmax(s, axis=-1)
        a = p[:, :, 0] - lam_full * p[:, :, 1]
        return jnp.einsum('bhqk,bkhe->bqhe', a.astype(v.dtype), v)

    o = lax.map(block, (jnp.arange(nblk), qb))
    o = o.transpose(1, 0, 2, 3, 4).reshape(bsz, t, B_HEADS, B_V_DIM)
    o = rmsnorm(o, b_norm_g.reshape(B_HEADS, B_V_DIM)) * (1.0 - lam_init)
    return o.reshape(bsz, t, B_WIDTH)


def peer(x, w_pq, sub_keys, u_tab, v_tab):
    lead = x.shape[:-1]
    xt = x.reshape(-1, D_MODEL)
    ntok = xt.shape[0]
    q = (xt @ w_pq).reshape(ntok, P_HEADS, 2, P_HALF)
    s = jnp.einsum('nhcd,hckd->nhck', q, sub_keys, preferred_element_type=jnp.float32)
    s1, i1 = lax.top_k(s[:, :, 0], P_TOPK)
    s2, i2 = lax.top_k(s[:, :, 1], P_TOPK)
    cand = (s1[..., :, None] + s2[..., None, :]).reshape(ntok, P_HEADS, P_TOPK * P_TOPK)
    cidx = (i1[..., :, None] * P_NKEYS + i2[..., None, :]).reshape(ntok, P_HEADS, P_TOPK * P_TOPK)
    top_s, pos = lax.top_k(cand, P_TOPK)
    idx = jnp.take_along_axis(cidx, pos, axis=-1)
    gates = jax.nn.softmax(top_s, axis=-1)
    nb = ntok // P_BLOCK

    def blk(args):
        xb, ib, gb = args
        hid = jax.nn.gelu(jnp.einsum('td,thkd->thk', xb, u_tab[ib]), approximate=False)
        w = (hid.astype(jnp.float32) * gb).astype(xb.dtype)
        return jnp.einsum('thk,thkd->td', w, v_tab[ib])

    y = lax.map(blk, (xt.reshape(nb, P_BLOCK, D_MODEL), idx.reshape(nb, P_BLOCK, P_HEADS, P_TOPK), gates.reshape(nb, P_BLOCK, P_HEADS, P_TOPK)))
    return y.reshape(*lead, D_MODEL).astype(x.dtype)


def _encoder(x, norm1_g, w_in, lb_logits, a_norm_g, w_a_proj, lam, b_norm_g, w_b_proj, w_o, norm2_g, w_pq, sub_keys, u_experts, v_experts, final_g):
    bsz, t, _ = x.shape
    f32 = jnp.float32
    lb_all = jnp.cumsum(jax.nn.softmax(lb_logits.astype(f32), axis=1), axis=1)
    for l in range(DEPTH):
        h = rmsnorm(x, norm1_g[l])
        z = h @ w_in[l]
        qa, ia, zf, zb, ga, qb, kb, vb, gate = _split(z, IN_SIZES)
        ya = hgrn2_mixer(qa, ia, zf, zb, ga, lb_all[:, l], a_norm_g[l]).astype(x.dtype) @ w_a_proj[l]
        lam_init = 0.8 - 0.6 * math.exp(-0.3 * l)
        yb = diff_attention(qb, kb, vb, lam[l], lam_init, b_norm_g[l]).astype(x.dtype) @ w_b_proj[l]
        g = jax.nn.sigmoid(gate.astype(f32)).reshape(bsz, t, 2, D_MODEL)
        mixed = g[:, :, 0] * ya.astype(f32) + g[:, :, 1] * yb.astype(f32)
        x = x + mixed.astype(x.dtype) @ w_o[l]
        x = x + peer(rmsnorm(x, norm2_g[l]), w_pq[l], sub_keys[l], u_experts[l], v_experts[l])
    return rmsnorm(x, final_g)


def setup_inputs(seed: int = 0) -> dict:
    key = jax.random.key(seed)
    ks = jax.random.split(key, 17)
    nrm = jax.random.normal
    f32 = jnp.float32
    return {
        "x_prompt": nrm(ks[0], (BATCH, SEQ, D_MODEL), f32),
        "x_sample": nrm(ks[1], (DEC_BATCH, DEC_SEQ, D_MODEL), f32),
        "norm1_g": 1.0 + 0.1 * nrm(ks[2], (DEPTH, D_MODEL), f32),
        "w_in": nrm(ks[3], (DEPTH, D_MODEL, IN_COLS), f32) * D_MODEL ** -0.5,
        "lb_logits": 0.5 * nrm(ks[4], (2, DEPTH + 1, A_WIDTH), f32),
        "a_norm_g": 1.0 + 0.1 * nrm(ks[5], (DEPTH, A_WIDTH), f32),
        "w_a_proj": nrm(ks[6], (DEPTH, A_WIDTH, D_MODEL), f32) * A_WIDTH ** -0.5,
        "lam": 0.1 * nrm(ks[7], (DEPTH, 4, B_QK_DIM), f32),
        "b_norm_g": 1.0 + 0.1 * nrm(ks[8], (DEPTH, B_WIDTH), f32),
        "w_b_proj": nrm(ks[9], (DEPTH, B_WIDTH, D_MODEL), f32) * B_WIDTH ** -0.5,
        "w_o": nrm(ks[10], (DEPTH, D_MODEL, D_MODEL), f32) * D_MODEL ** -0.5,
        "norm2_g": 1.0 + 0.1 * nrm(ks[11], (DEPTH, D_MODEL), f32),
        "w_pq": nrm(ks[12], (DEPTH, D_MODEL, P_HEADS * P_QDIM), f32) * D_MODEL ** -0.5,
        "sub_keys": nrm(ks[13], (DEPTH, P_HEADS, 2, P_NKEYS, P_HALF), f32) * P_HALF ** -0.5,
        "u_experts": nrm(ks[14], (DEPTH, P_EXPERTS, D_MODEL), f32) * D_MODEL ** -0.5,
        "v_experts": nrm(ks[15], (DEPTH, P_EXPERTS, D_MODEL), f32) * P_HEADS ** -0.5,
        "final_g": 1.0 + 0.1 * nrm(ks[16], (D_MODEL,), f32),
    }


def reference(x_prompt, x_sample, norm1_g, w_in, lb_logits, a_norm_g, w_a_proj, lam, b_norm_g, w_b_proj, w_o, norm2_g, w_pq, sub_keys, u_experts, v_experts, final_g):
    y_prompt = _encoder(x_prompt, norm1_g, w_in, lb_logits, a_norm_g, w_a_proj, lam, b_norm_g, w_b_proj, w_o, norm2_g, w_pq, sub_keys, u_experts, v_experts, final_g)
    y_sample = _encoder(x_sample, norm1_g, w_in, lb_logits, a_norm_g, w_a_proj, lam, b_norm_g, w_b_proj, w_o, norm2_g, w_pq, sub_keys, u_experts, v_experts, final_g)
    return (y_prompt, y_sample)
```

```python
import functools
import math

import jax
import jax.numpy as jnp
from jax import lax
from jax.experimental import pallas as pl
from jax.experimental.pallas import tpu as pltpu

F32 = jnp.float32
BF16 = jnp.bfloat16

A_HEADS = 8
A_HEAD_DIM = 128
B_HEADS = 8
B_QK_DIM = 64
B_V_DIM = 128
P_HEADS = 8
P_NKEYS = 128
P_HALF = 128
P_TOPK = 16
EPS = 1e-6
LAM_INIT = 0.8 - 0.6 * math.exp(-0.3 * 0)

LANES = 128
SUBLANES = 8
VMEM_LIMIT_BYTES = 56 * 1024 * 1024

NEG_BIG = -1e30
HGRN_CHUNK = 64
HGRN_SUB = SUBLANES


def _cparams(sem):
    return pltpu.CompilerParams(dimension_semantics=sem, vmem_limit_bytes=VMEM_LIMIT_BYTES)


def _tile(n, want):
    t = min(n, want)
    assert n % t == 0, (n, want)
    return t


def _rmsnorm_kernel(x_ref, g_ref, o_ref):
    x = x_ref[...]
    ms = jnp.mean(x * x, axis=-1, keepdims=True)
    o_ref[...] = (x * lax.rsqrt(ms + EPS) * g_ref[...]).astype(o_ref.dtype)


def _rmsnorm(x, g, out_dtype, tm=512):
    n, d = x.shape
    tm = _tile(n, tm)
    return pl.pallas_call(
        _rmsnorm_kernel,
        grid=(n // tm,),
        in_specs=[pl.BlockSpec((tm, d), lambda i: (i, 0)), pl.BlockSpec((1, d), lambda i: (0, 0))],
        out_specs=pl.BlockSpec((tm, d), lambda i: (i, 0)),
        out_shape=jax.ShapeDtypeStruct((n, d), out_dtype),
        compiler_params=_cparams(("parallel",)),
        name="rmsnorm",
    )(x, g.reshape(1, d))


def _mm_kernel(a_ref, w_ref, o_ref):
    o_ref[...] = jnp.dot(a_ref[...], w_ref[...], preferred_element_type=F32).astype(o_ref.dtype)


def _mm_res_kernel(a_ref, w_ref, r_ref, o_ref):
    o_ref[...] = (r_ref[...] + jnp.dot(a_ref[...], w_ref[...], preferred_element_type=F32)).astype(o_ref.dtype)


def _matmul(a, w, out_dtype, residual=None, tm=1024, tn=1024, name="matmul"):
    n, k = a.shape
    m = w.shape[1]
    tm, tn = _tile(n, tm), _tile(m, tn)
    in_specs = [pl.BlockSpec((tm, k), lambda i, j: (i, 0)), pl.BlockSpec((k, tn), lambda i, j: (0, j))]
    args = [a, w]
    kern = _mm_kernel
    if residual is not None:
        in_specs.append(pl.BlockSpec((tm, tn), lambda i, j: (i, j)))
        args.append(residual)
        kern = _mm_res_kernel
    return pl.pallas_call(
        kern,
        grid=(n // tm, m // tn),
        in_specs=in_specs,
        out_specs=pl.BlockSpec((tm, tn), lambda i, j: (i, j)),
        out_shape=jax.ShapeDtypeStruct((n, m), out_dtype),
        compiler_params=_cparams(("parallel", "arbitrary")),
        name=name,
    )(*args)


def _mm_nt_kernel(wt_ref, a_ref, o_ref):
    o_ref[...] = lax.dot_general(
        wt_ref[...], a_ref[...], (((1,), (1,)), ((), ())), preferred_element_type=F32
    ).astype(o_ref.dtype)


def _matmul_nt_batched(wt, a, bsz, t, out_dtype, tm=1024, tn=1024, name="matmul_nt"):
    m, k = wt.shape
    tm, tn = _tile(t, tm), _tile(m, tn)
    nt = t // tm
    return pl.pallas_call(
        _mm_nt_kernel,
        grid=(bsz, nt, m // tn),
        in_specs=[
            pl.BlockSpec((tn, k), lambda b, i, j: (j, 0)),
            pl.BlockSpec((tm, k), lambda b, i, j: (b * nt + i, 0)),
        ],
        out_specs=pl.BlockSpec((None, tn, tm), lambda b, i, j: (b, j, i)),
        out_shape=jax.ShapeDtypeStruct((bsz, m, t), out_dtype),
        compiler_params=_cparams(("parallel", "parallel", "arbitrary")),
        name=name,
    )(wt, a)


def _kaug_kernel(a_ref, w_ref, o_ref, *, tm):
    i = pl.program_id(1)
    h = pl.program_id(2)
    k = jnp.dot(a_ref[...], w_ref[...], preferred_element_type=F32)
    slope = jnp.exp2(-(h + 1).astype(F32))
    pos = i * tm + lax.broadcasted_iota(jnp.int32, (tm, LANES), 0)
    lane = lax.broadcasted_iota(jnp.int32, (tm, LANES), 1)
    hi = (pos >> 6).astype(F32)
    lo = (pos & 63).astype(F32)
    aug = jnp.where(lane == 0, slope * 64.0 * hi,
                    jnp.where(lane == 1, slope * lo,
                              jnp.where(lane == 2, -64.0 * slope, jnp.where(lane == 3, -slope, 0.0))))
    o_ref[...] = jnp.concatenate([k, aug], axis=1).astype(o_ref.dtype)


def _kaug(xn, wk, bsz, t, tm=1024):
    n, d = xn.shape
    assert t <= 64 * 256
    tm = _tile(t, tm)
    nt = t // tm
    return pl.pallas_call(
        functools.partial(_kaug_kernel, tm=tm),
        grid=(bsz, nt, B_HEADS),
        in_specs=[
            pl.BlockSpec((tm, d), lambda b, i, h: (b * nt + i, 0)),
            pl.BlockSpec((d, 2 * B_QK_DIM), lambda b, i, h: (0, h)),
        ],
        out_specs=pl.BlockSpec((None, None, tm, 2 * LANES), lambda b, i, h: (b, h, i, 0)),
        out_shape=jax.ShapeDtypeStruct((bsz, B_HEADS, t, 2 * LANES), BF16),
        compiler_params=_cparams(("parallel", "parallel", "arbitrary")),
        name="kaug",
    )(xn, wk)


def _attn_kernel(qt_ref, k_ref, vt_ref, lam_ref, g_ref, o_ref, qa_sc, m_sc, l_sc, acc_sc, *, tq, nk):
    h = pl.program_id(1)
    qi = pl.program_id(2)
    slope = jnp.exp2(-(h + 1).astype(F32))

    qt = qt_ref[...].astype(F32) * (B_QK_DIM ** -0.5)
    row = lax.broadcasted_iota(jnp.int32, (LANES, tq), 0)
    ipos = qi * tq + lax.broadcasted_iota(jnp.int32, (LANES, tq), 1)
    ihi = (ipos >> 6).astype(F32)
    ilo = (ipos & 63).astype(F32)
    aug = jnp.where(row < 2, 1.0, jnp.where(row == 2, ihi, jnp.where(row == 3, ilo, 0.0)))
    for c in range(2):
        qm = jnp.where((row >= B_QK_DIM * c) & (row < B_QK_DIM * (c + 1)), qt, 0.0)
        qa_sc[c, 0] = jnp.concatenate([qm, aug], axis=0).astype(BF16)
        qa_sc[c, 1] = jnp.concatenate([qm, -aug], axis=0).astype(BF16)

    m_sc[...] = jnp.full(m_sc.shape, NEG_BIG, F32)
    l_sc[...] = jnp.zeros(l_sc.shape, F32)
    acc_sc[...] = jnp.zeros(acc_sc.shape, F32)

    def update(c, s, vt):
        m_old = m_sc[c]
        m_new = jnp.maximum(m_old, jnp.max(s, axis=0, keepdims=True))
        alpha = jnp.exp(m_old - m_new)
        p = jnp.exp(s - m_new)
        l_sc[c] = alpha * l_sc[c] + jnp.sum(p, axis=0, keepdims=True)
        acc_sc[c] = alpha * acc_sc[c] + jnp.dot(vt, p.astype(BF16), preferred_element_type=F32)
        m_sc[c] = m_new

    def side_tile(j, side):
        off = pl.multiple_of(j * tq, tq)
        kt = k_ref[pl.ds(off, tq), :]
        vt = vt_ref[:, pl.ds(off, tq)]
        for c in range(2):
            s = jnp.dot(kt, qa_sc[c, side], preferred_element_type=F32)
            update(c, s, vt)

    def left_body(j, carry):
        side_tile(j, 0)
        return carry

    def right_body(j, carry):
        side_tile(j, 1)
        return carry

    lax.fori_loop(0, qi, left_body, 0)

    off = pl.multiple_of(qi * tq, tq)
    kt = k_ref[pl.ds(off, tq), 0:LANES]
    vt = vt_ref[:, pl.ds(off, tq)]
    jj = lax.broadcasted_iota(jnp.int32, (tq, tq), 0)
    ii = lax.broadcasted_iota(jnp.int32, (tq, tq), 1)
    bias = -slope * jnp.abs(ii - jj).astype(F32)
    for c in range(2):
        s = jnp.dot(kt, qa_sc[c, 0, 0:LANES, :], preferred_element_type=F32) + bias
        update(c, s, vt)

    lax.fori_loop(qi + 1, nk, right_body, 0)

    lf = lam_ref[...]
    lam_full = (jnp.exp(jnp.sum(lf[0:1] * lf[1:2], axis=-1, keepdims=True))
                - jnp.exp(jnp.sum(lf[2:3] * lf[3:4], axis=-1, keepdims=True)) + LAM_INIT)
    o = acc_sc[0] / l_sc[0] - lam_full * (acc_sc[1] / l_sc[1])
    ms = jnp.mean(o * o, axis=0, keepdims=True)
    on = o * lax.rsqrt(ms + EPS)
    on = on.T * g_ref[...] * (1.0 - LAM_INIT)
    o_ref[...] = on.astype(o_ref.dtype)


def _attention(qv_t, kaug, lam, b_norm_g, bsz, t, tq=512):
    tq = _tile(t, tq)
    nq = t // tq
    kern = functools.partial(_attn_kernel, tq=tq, nk=nq)
    qv = qv_t.reshape(bsz * 2 * B_HEADS, LANES, t)
    return pl.pallas_call(
        kern,
        grid=(bsz, B_HEADS, nq),
        in_specs=[
            pl.BlockSpec((None, LANES, tq), lambda b, h, q: (b * 2 * B_HEADS + h, 0, q)),
            pl.BlockSpec((None, None, t, 2 * LANES), lambda b, h, q: (b, h, 0, 0)),
            pl.BlockSpec((None, LANES, t), lambda b, h, q: (b * 2 * B_HEADS + B_HEADS + h, 0, 0)),
            pl.BlockSpec((4, B_QK_DIM), lambda b, h, q: (0, 0)),
            pl.BlockSpec((1, B_V_DIM), lambda b, h, q: (0, h)),
        ],
        out_specs=pl.BlockSpec((tq, B_V_DIM), lambda b, h, q: (b * nq + q, h)),
        out_shape=jax.ShapeDtypeStruct((bsz * t, B_HEADS * B_V_DIM), BF16),
        scratch_shapes=[
            pltpu.VMEM((2, 2, 2 * LANES, tq), BF16),
            pltpu.VMEM((2, 1, tq), F32),
            pltpu.VMEM((2, 1, tq), F32),
            pltpu.VMEM((2, B_V_DIM, tq), F32),
        ],
        compiler_params=_cparams(("parallel", "parallel", "arbitrary")),
        name="diff_attention",
    )(qv, kaug, qv, lam, b_norm_g.reshape(1, -1))


def _hgrn_chunk(q, k, v, g, st_ref, rev):
    c, sub = HGRN_CHUNK, HGRN_SUB
    nb = c // sub
    r_i = lax.broadcasted_iota(jnp.int32, (c, c), 0)
    c_i = lax.broadcasted_iota(jnp.int32, (c, c), 1)
    tri = jnp.where((c_i >= r_i) if rev else (c_i <= r_i), 1.0, 0.0).astype(F32)
    b = jnp.dot(tri, g, preferred_element_type=F32, precision=lax.Precision.HIGHEST)
    btot = b[0:1] if rev else b[c - 1:c]
    st = st_ref[...]

    qe = q * jnp.exp(b)
    o = lax.dot_general(qe.astype(BF16), st.astype(BF16), (((1,), (1,)), ((), ())), preferred_element_type=F32)

    q_parts, k_parts = [], []
    for j in (range(1, nb) if rev else range(nb - 1)):
        if rev:
            ej = b[sub * j:sub * j + 1]
            qrows = slice(0, sub * j)
        else:
            ej = b[sub * j + sub - 1:sub * j + sub]
            qrows = slice(sub * (j + 1), c)
        qp = q[qrows] * jnp.exp(b[qrows] - ej)
        kp = k[sub * j:sub * (j + 1)] * jnp.exp(ej - b[sub * j:sub * (j + 1)])
        zq = jnp.zeros((c - qp.shape[0], LANES), F32)
        q_parts.append(jnp.concatenate([qp, zq] if rev else [zq, qp], axis=0))
        pieces = []
        if j > 0:
            pieces.append(jnp.zeros((sub * j, LANES), F32))
        pieces.append(kp)
        if j < nb - 1:
            pieces.append(jnp.zeros((c - sub * (j + 1), LANES), F32))
        k_parts.append(jnp.concatenate(pieces, axis=0))
    qcat = jnp.concatenate(q_parts, axis=1).astype(BF16)
    kcat = jnp.concatenate(k_parts, axis=1).astype(BF16)
    sc = lax.dot_general(qcat, kcat, (((1,), (1,)), ((), ())), preferred_element_type=F32)
    o = o + jnp.dot(sc.astype(BF16), v.astype(BF16), preferred_element_type=F32)

    b3, q3, k3, v3 = (a.reshape(nb, sub, LANES) for a in (b, q, k, v))
    srow = lax.broadcasted_iota(jnp.int32, (1, sub, 1), 1)
    acc = jnp.zeros((nb, sub, LANES), F32)
    for j in range(sub):
        keep = (srow <= j) if rev else (srow >= j)
        w = jnp.exp(jnp.where(keep, b3 - b3[:, j:j + 1, :], NEG_BIG))
        sj = jnp.sum(q3 * w * k3[:, j:j + 1, :], axis=-1, keepdims=True)
        acc = acc + sj * v3[:, j:j + 1, :]
    o = o + acc.reshape(c, LANES)

    kd = k * jnp.exp(btot - b)
    st_ref[...] = st * jnp.exp(btot) + jnp.dot(v.T.astype(BF16), kd.astype(BF16), preferred_element_type=F32)
    return o


def _hgrn_kernel(qf_ref, if_ref, zf_ref, qb_ref, ib_ref, zb_ref, lb_ref, of_ref, ob_ref, st_sc, *, ts):
    @pl.when(pl.program_id(2) == 0)
    def _():
        st_sc[...] = jnp.zeros(st_sc.shape, F32)

    lg = lb_ref[...]
    lmax = jnp.max(lg, axis=1, keepdims=True)
    le = jnp.exp(lg - lmax)
    lb = le[:, 0, :] / jnp.sum(le, axis=1)

    nc = ts // HGRN_CHUNK

    def gates(qa, ia, z, lbd):
        f = lbd + (1.0 - lbd) * jax.nn.sigmoid(z)
        return qa * jax.nn.sigmoid(qa), 1.0 - f, ia, jnp.log(f)

    def body(ci, carry):
        off = pl.multiple_of(ci * HGRN_CHUNK, HGRN_CHUNK)
        rows = pl.ds(off, HGRN_CHUNK)
        q, k, v, g = gates(qf_ref[rows, :], if_ref[rows, :], zf_ref[rows, :], lb[0:1])
        of_ref[rows, :] = _hgrn_chunk(q, k, v, g, st_sc.at[0], rev=False)
        roff = pl.multiple_of((nc - 1 - ci) * HGRN_CHUNK, HGRN_CHUNK)
        rrows = pl.ds(roff, HGRN_CHUNK)
        q, k, v, g = gates(qb_ref[rrows, :], ib_ref[rrows, :], zb_ref[rrows, :], lb[1:2])
        ob_ref[rrows, :] = _hgrn_chunk(q, k, v, g, st_sc.at[1], rev=True)
        return carry

    lax.fori_loop(0, nc, body, 0)


def _hgrn(z_nat, lb_logits, bsz, t, ts=512):
    assert lb_logits.shape[1] == 2
    n = bsz * t
    ts = _tile(t, ts)
    ns = t // ts
    hd = A_HEAD_DIM

    def fwd(col):
        return pl.BlockSpec((ts, hd), lambda b, h, i: (b * ns + i, col * A_HEADS + h))

    def bwd(col):
        return pl.BlockSpec((ts, hd), lambda b, h, i: (b * ns + ns - 1 - i, col * A_HEADS + h))

    return pl.pallas_call(
        functools.partial(_hgrn_kernel, ts=ts),
        grid=(bsz, A_HEADS, ns),
        in_specs=[fwd(0), fwd(1), fwd(2), bwd(0), bwd(1), bwd(3),
                  pl.BlockSpec((2, 2, hd), lambda b, h, i: (0, 0, h))],
        out_specs=[
            pl.BlockSpec((ts, hd), lambda b, h, i: (b * ns + i, h)),
            pl.BlockSpec((ts, hd), lambda b, h, i: (b * ns + ns - 1 - i, h)),
        ],
        out_shape=[jax.ShapeDtypeStruct((n, A_HEADS * hd), F32)] * 2,
        scratch_shapes=[pltpu.VMEM((2, hd, hd), F32)],
        compiler_params=_cparams(("parallel", "parallel", "arbitrary")),
        name="hgrn2",
    )(z_nat, z_nat, z_nat, z_nat, z_nat, z_nat, lb_logits)


def _merge_kernel(of_ref, ob_ref, ga_ref, ag_ref, on_ref, gate_ref, wa_ref, wb_ref, o_ref, *, d):
    o = of_ref[...] + ob_ref[...]
    parts = []
    for h in range(A_HEADS):
        oh = o[:, h * A_HEAD_DIM:(h + 1) * A_HEAD_DIM]
        ms = jnp.mean(oh * oh, axis=-1, keepdims=True)
        parts.append(oh * lax.rsqrt(ms + EPS))
    ga = ga_ref[...]
    oa = jnp.concatenate(parts, axis=1) * ag_ref[...] * (ga * jax.nn.sigmoid(ga))
    ya = jnp.dot(oa.astype(BF16), wa_ref[...], preferred_element_type=F32)
    yb = jnp.dot(on_ref[...], wb_ref[...], preferred_element_type=F32)
    gts = jax.nn.sigmoid(gate_ref[...])
    o_ref[...] = (gts[:, :d] * ya + gts[:, d:] * yb).astype(o_ref.dtype)


def _merge(o_f, o_b, z_nat, a_norm_g, on, gate, w_a, w_b, tm=256):
    n, aw = o_f.shape
    d = w_a.shape[1]
    tm = _tile(n, tm)
    ga_col = 4 * A_HEADS * A_HEAD_DIM // aw
    return pl.pallas_call(
        functools.partial(_merge_kernel, d=d),
        grid=(n // tm,),
        in_specs=[
            pl.BlockSpec((tm, aw), lambda i: (i, 0)),
            pl.BlockSpec((tm, aw), lambda i: (i, 0)),
            pl.BlockSpec((tm, aw), lambda i: (i, ga_col)),
            pl.BlockSpec((1, aw), lambda i: (0, 0)),
            pl.BlockSpec((tm, on.shape[1]), lambda i: (i, 0)),
            pl.BlockSpec((tm, 2 * d), lambda i: (i, 0)),
            pl.BlockSpec(w_a.shape, lambda i: (0, 0)),
            pl.BlockSpec(w_b.shape, lambda i: (0, 0)),
        ],
        out_specs=pl.BlockSpec((tm, d), lambda i: (i, 0)),
        out_shape=jax.ShapeDtypeStruct((n, d), BF16),
        compiler_params=_cparams(("parallel",)),
        name="merge",
    )(o_f, o_b, z_nat, a_norm_g.reshape(1, aw), on, gate, w_a, w_b)


def _keyproj_kernel(keys_ref, wpq_ref, o_ref):
    o_ref[...] = lax.dot_general(
        keys_ref[...], wpq_ref[...], (((1,), (1,)), ((), ())),
        preferred_element_type=F32, precision=lax.Precision.HIGHEST,
    ).astype(o_ref.dtype)


def _keyproj(sub_keys, w_pq):
    d = w_pq.shape[0]
    nhc = 2 * P_HEADS
    keys = sub_keys.reshape(nhc, P_NKEYS, P_HALF)
    return pl.pallas_call(
        _keyproj_kernel,
        grid=(nhc,),
        in_specs=[
            pl.BlockSpec((None, P_NKEYS, P_HALF), lambda i: (i, 0, 0)),
            pl.BlockSpec((d, P_HALF), lambda i: (0, i)),
        ],
        out_specs=pl.BlockSpec((P_NKEYS, d), lambda i: (i, 0)),
        out_shape=jax.ShapeDtypeStruct((nhc * P_NKEYS, d), BF16),
        compiler_params=_cparams(("parallel",)),
        name="peer_keyproj",
    )(keys, w_pq)


def _topk_kernel(s_ref, tau_ref, c1_ref, e2_ref, *, tm):
    s1 = s_ref[0:P_NKEYS, :]
    s2 = s_ref[P_NKEYS:2 * P_NKEYS, :]

    def top_sorted(s):
        vals = []
        work = s
        for _ in range(P_TOPK):
            mx = jnp.max(work, axis=0, keepdims=True)
            vals.append(mx)
            work = jnp.where(work == mx, NEG_BIG, work)
        return vals

    u = top_sorted(s1)
    v = top_sorted(s2)
    cands = [u[i] + v[j] for i in range(P_TOPK) for j in range(P_TOPK) if (i + 1) * (j + 1) <= P_TOPK]
    nrow = -(-len(cands) // SUBLANES) * SUBLANES
    crow = lax.broadcasted_iota(jnp.int32, (nrow, tm), 0)
    cand = jnp.full((nrow, tm), NEG_BIG, F32)
    for idx, cv in enumerate(cands):
        cand = jnp.where(crow == idx, cv, cand)
    top = u[0] + v[0]
    z = jnp.zeros((1, tm), F32)
    tau = top
    for _ in range(P_TOPK):
        mx = jnp.max(cand, axis=0, keepdims=True)
        z = z + jnp.exp(mx - top)
        tau = mx
        cand = jnp.where(cand == mx, NEG_BIG, cand)
    tau_ref[...] = jnp.broadcast_to(tau, tau_ref.shape)
    c1_ref[...] = jnp.exp(s1 - u[0]) / z
    e2_ref[...] = jnp.exp(s2 - v[0])


def _topk(st, tm=256):
    rows, n = st.shape
    tm = _tile(n, tm)
    return pl.pallas_call(
        functools.partial(_topk_kernel, tm=tm),
        grid=(n // tm, P_HEADS),
        in_specs=[pl.BlockSpec((2 * P_NKEYS, tm), lambda i, h: (h, i))],
        out_specs=[
            pl.BlockSpec((None, SUBLANES, tm), lambda i, h: (h, 0, i)),
            pl.BlockSpec((P_NKEYS, tm), lambda i, h: (h, i)),
            pl.BlockSpec((P_NKEYS, tm), lambda i, h: (h, i)),
        ],
        out_shape=[
            jax.ShapeDtypeStruct((P_HEADS, SUBLANES, n), F32),
            jax.ShapeDtypeStruct((P_HEADS * P_NKEYS, n), F32),
            jax.ShapeDtypeStruct((P_HEADS * P_NKEYS, n), F32),
        ],
        compiler_params=_cparams(("parallel", "arbitrary")),
        name="peer_topk",
    )(st)


def _peer_kernel(xn_ref, u_ref, vt_ref, s1_ref, s2_ref, tau_ref, c1_ref, e2_ref, y_ref, w_sc, *, tm, te):
    e = pl.program_id(1)

    @pl.when(e == 0)
    def _():
        y_ref[...] = jnp.zeros(y_ref.shape, F32)

    hid = lax.dot_general(u_ref[...], xn_ref[...], (((1,), (1,)), ((), ())), preferred_element_type=F32)
    for al in range(te // P_NKEYS):
        for lc in range(tm // LANES):
            cols = slice(lc * LANES, (lc + 1) * LANES)
            acc = jnp.zeros((P_NKEYS, LANES), F32)
            for h in range(P_HEADS):
                keep = (s1_ref[h, al:al + 1, cols] + s2_ref[h, :, cols]) >= tau_ref[h, 0:1, cols]
                gate = e2_ref[h, :, cols] * c1_ref[h, al:al + 1, cols]
                acc = acc + jnp.where(keep, gate, 0.0)
            hv = hid[al * P_NKEYS:(al + 1) * P_NKEYS, cols]
            gelu = 0.5 * hv * (1.0 + lax.erf(hv * (2.0 ** -0.5)))
            w_sc[al * P_NKEYS:(al + 1) * P_NKEYS, cols] = (acc * gelu).astype(BF16)
    y_ref[...] += jnp.dot(vt_ref[...], w_sc[...], preferred_element_type=F32)


def _peer_dense(xn, u_bf, vt_bf, st, tau, c1, e2, tm=512):
    n, d = xn.shape
    nexp = u_bf.shape[0]
    te = SUBLANES * P_NKEYS
    tm = _tile(n, tm)
    assert nexp == P_NKEYS * P_NKEYS and tm % LANES == 0
    st4 = st.reshape(P_HEADS, 2, P_NKEYS, n)
    c13 = c1.reshape(P_HEADS, P_NKEYS, n)
    e23 = e2.reshape(P_HEADS, P_NKEYS, n)
    return pl.pallas_call(
        functools.partial(_peer_kernel, tm=tm, te=te),
        grid=(n // tm, nexp // te),
        in_specs=[
            pl.BlockSpec((tm, d), lambda i, e: (i, 0)),
            pl.BlockSpec((te, d), lambda i, e: (e, 0)),
            pl.BlockSpec((d, te), lambda i, e: (0, e)),
            pl.BlockSpec((P_HEADS, None, SUBLANES, tm), lambda i, e: (0, 0, e, i)),
            pl.BlockSpec((P_HEADS, None, P_NKEYS, tm), lambda i, e: (0, 1, 0, i)),
            pl.BlockSpec((P_HEADS, SUBLANES, tm), lambda i, e: (0, 0, i)),
            pl.BlockSpec((P_HEADS, SUBLANES, tm), lambda i, e: (0, e, i)),
            pl.BlockSpec((P_HEADS, P_NKEYS, tm), lambda i, e: (0, 0, i)),
        ],
        out_specs=pl.BlockSpec((d, tm), lambda i, e: (0, i)),
        out_shape=jax.ShapeDtypeStruct((d, n), F32),
        scratch_shapes=[pltpu.VMEM((te, tm), BF16)],
        compiler_params=_cparams(("parallel", "arbitrary")),
        name="peer_dense",
    )(xn, u_bf, vt_bf, st4, st4, tau, c13, e23)


def _final_kernel(x_ref, yt_ref, g_ref, o_ref):
    x = x_ref[...] + yt_ref[...].T
    ms = jnp.mean(x * x, axis=-1, keepdims=True)
    o_ref[...] = x * lax.rsqrt(ms + EPS) * g_ref[...]


def _final(x1, yt, g, tm=256):
    n, d = x1.shape
    tm = _tile(n, tm)
    return pl.pallas_call(
        _final_kernel,
        grid=(n // tm,),
        in_specs=[
            pl.BlockSpec((tm, d), lambda i: (i, 0)),
            pl.BlockSpec((d, tm), lambda i: (0, i)),
            pl.BlockSpec((1, d), lambda i: (0, 0)),
        ],
        out_specs=pl.BlockSpec((tm, d), lambda i: (i, 0)),
        out_shape=jax.ShapeDtypeStruct((n, d), F32),
        compiler_params=_cparams(("parallel",)),
        name="final_norm",
    )(x1, yt, g.reshape(1, d))


def _prepare_weights(w_in, w_a_proj, w_b_proj, w_o, w_pq, sub_keys, u_experts, v_experts):
    aw = A_HEADS * A_HEAD_DIM
    qk = 2 * B_HEADS * B_QK_DIM
    bw = B_HEADS * B_V_DIM
    w = w_in[0]
    c0 = 5 * aw
    return dict(
        w_nat=w[:, :c0].astype(BF16),
        w_qv_t=jnp.concatenate([w[:, c0:c0 + qk], w[:, c0 + 2 * qk:c0 + 2 * qk + bw]], axis=1).T.astype(BF16),
        w_k=w[:, c0 + qk:c0 + 2 * qk].astype(BF16),
        w_gate=w[:, c0 + 2 * qk + bw:].astype(BF16),
        w_a=w_a_proj[0].astype(BF16),
        w_b=w_b_proj[0].astype(BF16),
        w_o=w_o[0].astype(BF16),
        keyproj=_keyproj(sub_keys[0], w_pq[0]),
        u=u_experts[0].astype(BF16),
        vt=v_experts[0].T.astype(BF16),
    )


def _encoder(x, wts, norm1_g, lb_logits, a_norm_g, lam, b_norm_g, norm2_g, final_g):
    bsz, t, d = x.shape
    n = bsz * t
    x2 = x.reshape(n, d)
    qk = 2 * B_HEADS * B_QK_DIM

    xn = _rmsnorm(x2, norm1_g[0], BF16)
    z_nat = _matmul(xn, wts["w_nat"], F32, name="inproj_nat")
    gate = _matmul(xn, wts["w_gate"], F32, name="inproj_gate")
    qv_t = _matmul_nt_batched(wts["w_qv_t"], xn, bsz, t, BF16, name="inproj_qv_t")
    kaug = _kaug(xn, wts["w_k"], bsz, t)

    o_f, o_b = _hgrn(z_nat, lb_logits, bsz, t)
    on = _attention(qv_t, kaug, lam[0], b_norm_g[0], bsz, t)

    mixed = _merge(o_f, o_b, z_nat, a_norm_g[0], on, gate, wts["w_a"], wts["w_b"])
    x1 = _matmul(mixed, wts["w_o"], F32, residual=x2, name="out_proj")

    xn2 = _rmsnorm(x1, norm2_g[0], BF16)
    st = _matmul_nt_batched(wts["keyproj"], xn2, 1, n, F32, name="peer_scores")[0]
    tau, c1, e2 = _topk(st)
    yt = _peer_dense(xn2, wts["u"], wts["vt"], st, tau, c1, e2)
    return _final(x1, yt, final_g).reshape(bsz, t, d)


def kernel(x_prompt, x_sample, norm1_g, w_in, lb_logits, a_norm_g, w_a_proj, lam, b_norm_g, w_b_proj, w_o, norm2_g, w_pq, sub_keys, u_experts, v_experts, final_g):
    assert w_in.shape[0] == 1
    wts = _prepare_weights(w_in, w_a_proj, w_b_proj, w_o, w_pq, sub_keys, u_experts, v_experts)
    args = (wts, norm1_g, lb_logits, a_norm_g, lam, b_norm_g, norm2_g, final_g)
    return (_encoder(x_prompt, *args), _encoder(x_sample, *args))
```

```python
import functools
import math

import jax
import jax.numpy as jnp
from jax import lax
from jax.experimental import pallas as pl
from jax.experimental.pallas import tpu as pltpu

F32 = jnp.float32
BF16 = jnp.bfloat16

A_HEADS = 8
A_HEAD_DIM = 128
B_HEADS = 8
B_QK_DIM = 64
B_V_DIM = 128
P_HEADS = 8
P_NKEYS = 128
P_HALF = 128
P_TOPK = 16
EPS = 1e-6
LAM_INIT = 0.8 - 0.6 * math.exp(-0.3 * 0)

LANES = 128
SUBLANES = 8
VMEM_LIMIT_BYTES = 56 * 1024 * 1024

NEG_BIG = -1e30
HGRN_CHUNK = 64
HGRN_SUB = SUBLANES
HGRN_HEADS = 4


def _cparams(sem, flags=None):
    return pltpu.CompilerParams(dimension_semantics=sem, vmem_limit_bytes=VMEM_LIMIT_BYTES, flags=flags)


def _tile(n, want):
    t = min(n, want)
    assert n % t == 0, (n, want)
    return t


def _rmsnorm_kernel(x_ref, g_ref, o_ref):
    x = x_ref[...]
    ms = jnp.mean(x * x, axis=-1, keepdims=True)
    o_ref[...] = (x * lax.rsqrt(ms + EPS) * g_ref[...]).astype(o_ref.dtype)


def _rmsnorm(x, g, out_dtype, tm=512):
    n, d = x.shape
    tm = _tile(n, tm)
    return pl.pallas_call(
        _rmsnorm_kernel,
        grid=(n // tm,),
        in_specs=[pl.BlockSpec((tm, d), lambda i: (i, 0)), pl.BlockSpec((1, d), lambda i: (0, 0))],
        out_specs=pl.BlockSpec((tm, d), lambda i: (i, 0)),
        out_shape=jax.ShapeDtypeStruct((n, d), out_dtype),
        compiler_params=_cparams(("parallel",)),
        name="rmsnorm",
    )(x, g.reshape(1, d))


def _rmsnorm_t_kernel(x_ref, g_ref, o_ref):
    x = x_ref[...]
    ms = jnp.mean(x * x, axis=-1, keepdims=True)
    o_ref[...] = (x * lax.rsqrt(ms + EPS) * g_ref[...]).T.astype(o_ref.dtype)


def _rmsnorm_t(x, g, out_dtype, tm=512):
    n, d = x.shape
    tm = _tile(n, tm)
    return pl.pallas_call(
        _rmsnorm_t_kernel,
        grid=(n // tm,),
        in_specs=[pl.BlockSpec((tm, d), lambda i: (i, 0)), pl.BlockSpec((1, d), lambda i: (0, 0))],
        out_specs=pl.BlockSpec((d, tm), lambda i: (0, i)),
        out_shape=jax.ShapeDtypeStruct((d, n), out_dtype),
        compiler_params=_cparams(("parallel",)),
        name="rmsnorm_t",
    )(x, g.reshape(1, d))


def _mm_kernel(a_ref, w_ref, o_ref):
    o_ref[...] = jnp.dot(a_ref[...], w_ref[...], preferred_element_type=F32).astype(o_ref.dtype)


def _mm_res_kernel(a_ref, w_ref, r_ref, o_ref):
    o_ref[...] = (r_ref[...] + jnp.dot(a_ref[...], w_ref[...], preferred_element_type=F32)).astype(o_ref.dtype)


def _matmul(a, w, out_dtype, residual=None, tm=1024, tn=1024, name="matmul"):
    n, k = a.shape
    m = w.shape[1]
    tm, tn = _tile(n, tm), _tile(m, tn)
    in_specs = [pl.BlockSpec((tm, k), lambda i, j: (i, 0)), pl.BlockSpec((k, tn), lambda i, j: (0, j))]
    args = [a, w]
    kern = _mm_kernel
    if residual is not None:
        in_specs.append(pl.BlockSpec((tm, tn), lambda i, j: (i, j)))
        args.append(residual)
        kern = _mm_res_kernel
    return pl.pallas_call(
        kern,
        grid=(n // tm, m // tn),
        in_specs=in_specs,
        out_specs=pl.BlockSpec((tm, tn), lambda i, j: (i, j)),
        out_shape=jax.ShapeDtypeStruct((n, m), out_dtype),
        compiler_params=_cparams(("parallel", "arbitrary")),
        name=name,
    )(*args)


def _mm_nt_kernel(wt_ref, a_ref, o_ref):
    o_ref[...] = lax.dot_general(
        wt_ref[...], a_ref[...], (((1,), (1,)), ((), ())), preferred_element_type=F32
    ).astype(o_ref.dtype)


def _matmul_nt_batched(wt, a, bsz, t, out_dtype, tm=1024, tn=1024, name="matmul_nt"):
    m, k = wt.shape
    tm, tn = _tile(t, tm), _tile(m, tn)
    nt = t // tm
    return pl.pallas_call(
        _mm_nt_kernel,
        grid=(bsz, nt, m // tn),
        in_specs=[
            pl.BlockSpec((tn, k), lambda b, i, j: (j, 0)),
            pl.BlockSpec((tm, k), lambda b, i, j: (b * nt + i, 0)),
        ],
        out_specs=pl.BlockSpec((None, tn, tm), lambda b, i, j: (b, j, i)),
        out_shape=jax.ShapeDtypeStruct((bsz, m, t), out_dtype),
        compiler_params=_cparams(("parallel", "parallel", "arbitrary")),
        name=name,
    )(wt, a)


def _kaug_kernel(a_ref, w_ref, o_ref, *, tm):
    i = pl.program_id(1)
    h = pl.program_id(2)
    k = jnp.dot(a_ref[...], w_ref[...], preferred_element_type=F32)
    slope = jnp.exp2(-(h + 1).astype(F32))
    pos = i * tm + lax.broadcasted_iota(jnp.int32, (tm, LANES), 0)
    lane = lax.broadcasted_iota(jnp.int32, (tm, LANES), 1)
    hi = (pos >> 6).astype(F32)
    lo = (pos & 63).astype(F32)
    aug = jnp.where(lane == 0, slope * 64.0 * hi,
                    jnp.where(lane == 1, slope * lo,
                              jnp.where(lane == 2, -64.0 * slope, jnp.where(lane == 3, -slope, 0.0))))
    o_ref[...] = jnp.concatenate([k, aug], axis=1).astype(o_ref.dtype)


def _kaug(xn, wk, bsz, t, tm=1024):
    n, d = xn.shape
    assert t <= 64 * 256
    tm = _tile(t, tm)
    nt = t // tm
    return pl.pallas_call(
        functools.partial(_kaug_kernel, tm=tm),
        grid=(bsz, nt, B_HEADS),
        in_specs=[
            pl.BlockSpec((tm, d), lambda b, i, h: (b * nt + i, 0)),
            pl.BlockSpec((d, 2 * B_QK_DIM), lambda b, i, h: (0, h)),
        ],
        out_specs=pl.BlockSpec((None, None, tm, 2 * LANES), lambda b, i, h: (b, h, i, 0)),
        out_shape=jax.ShapeDtypeStruct((bsz, B_HEADS, t, 2 * LANES), BF16),
        compiler_params=_cparams(("parallel", "parallel", "arbitrary")),
        name="kaug",
    )(xn, wk)


def _attn_kernel(qt_ref, k_ref, vt_ref, lam_ref, g_ref, o_ref, qa_sc, m_sc, l_sc, acc_sc, *, tq, nk):
    h = pl.program_id(1)
    qi = pl.program_id(2)
    slope = jnp.exp2(-(h + 1).astype(F32))

    qt = qt_ref[...].astype(F32) * (B_QK_DIM ** -0.5)
    row = lax.broadcasted_iota(jnp.int32, (LANES, tq), 0)
    ipos = qi * tq + lax.broadcasted_iota(jnp.int32, (LANES, tq), 1)
    ihi = (ipos >> 6).astype(F32)
    ilo = (ipos & 63).astype(F32)
    aug = jnp.where(row < 2, 1.0, jnp.where(row == 2, ihi, jnp.where(row == 3, ilo, 0.0)))
    for c in range(2):
        qm = jnp.where((row >= B_QK_DIM * c) & (row < B_QK_DIM * (c + 1)), qt, 0.0)
        qa_sc[c, 0] = jnp.concatenate([qm, aug], axis=0).astype(BF16)
        qa_sc[c, 1] = jnp.concatenate([qm, -aug], axis=0).astype(BF16)

    m_sc[...] = jnp.full(m_sc.shape, NEG_BIG, F32)
    l_sc[...] = jnp.zeros(l_sc.shape, F32)
    acc_sc[...] = jnp.zeros(acc_sc.shape, F32)

    def update(c, s, vt):
        m_old = m_sc[c]
        m_new = jnp.maximum(m_old, jnp.max(s, axis=0, keepdims=True))
        alpha = jnp.exp(m_old - m_new)
        p = jnp.exp(s - m_new)
        l_sc[c] = alpha * l_sc[c] + jnp.sum(p, axis=0, keepdims=True)
        acc_sc[c] = alpha * acc_sc[c] + jnp.dot(vt, p.astype(BF16), preferred_element_type=F32)
        m_sc[c] = m_new

    def side_tile(jj, carry):
        j = jj + (jj >= qi).astype(jnp.int32)
        side = (j > qi).astype(jnp.int32)
        off = pl.multiple_of(j * tq, tq)
        kt = k_ref[pl.ds(off, tq), :]
        vt = vt_ref[:, pl.ds(off, tq)]
        for c in range(2):
            s = jnp.dot(kt, qa_sc[c, side], preferred_element_type=F32)
            update(c, s, vt)
        return carry

    lax.fori_loop(0, nk - 1, side_tile, 0)

    off = pl.multiple_of(qi * tq, tq)
    kt = k_ref[pl.ds(off, tq), 0:LANES]
    vt = vt_ref[:, pl.ds(off, tq)]
    jj = lax.broadcasted_iota(jnp.int32, (tq, tq), 0)
    ii = lax.broadcasted_iota(jnp.int32, (tq, tq), 1)
    bias = -slope * jnp.abs(ii - jj).astype(F32)
    for c in range(2):
        s = jnp.dot(kt, qa_sc[c, 0, 0:LANES, :], preferred_element_type=F32) + bias
        update(c, s, vt)

    lf = lam_ref[...]
    lam_full = (jnp.exp(jnp.sum(lf[0:1] * lf[1:2], axis=-1, keepdims=True))
                - jnp.exp(jnp.sum(lf[2:3] * lf[3:4], axis=-1, keepdims=True)) + LAM_INIT)
    o = acc_sc[0] / l_sc[0] - lam_full * (acc_sc[1] / l_sc[1])
    ms = jnp.mean(o * o, axis=0, keepdims=True)
    on = o * lax.rsqrt(ms + EPS)
    on = on.T * g_ref[...] * (1.0 - LAM_INIT)
    o_ref[...] = on.astype(o_ref.dtype)


def _attention(qv_t, kaug, lam, b_norm_g, bsz, t, tq=1024):
    tq = _tile(t, tq)
    nq = t // tq
    kern = functools.partial(_attn_kernel, tq=tq, nk=nq)
    qv = qv_t.reshape(bsz * 2 * B_HEADS, LANES, t)
    return pl.pallas_call(
        kern,
        grid=(bsz, B_HEADS, nq),
        in_specs=[
            pl.BlockSpec((None, LANES, tq), lambda b, h, q: (b * 2 * B_HEADS + h, 0, q)),
            pl.BlockSpec((None, None, t, 2 * LANES), lambda b, h, q: (b, h, 0, 0)),
            pl.BlockSpec((None, LANES, t), lambda b, h, q: (b * 2 * B_HEADS + B_HEADS + h, 0, 0)),
            pl.BlockSpec((4, B_QK_DIM), lambda b, h, q: (0, 0)),
            pl.BlockSpec((1, B_V_DIM), lambda b, h, q: (0, h)),
        ],
        out_specs=pl.BlockSpec((tq, B_V_DIM), lambda b, h, q: (b * nq + q, h)),
        out_shape=jax.ShapeDtypeStruct((bsz * t, B_HEADS * B_V_DIM), BF16),
        scratch_shapes=[
            pltpu.VMEM((2, 2, 2 * LANES, tq), BF16),
            pltpu.VMEM((2, 1, tq), F32),
            pltpu.VMEM((2, 1, tq), F32),
            pltpu.VMEM((2, B_V_DIM, tq), F32),
        ],
        compiler_params=_cparams(("parallel", "parallel", "arbitrary")),
        name="diff_attention",
    )(qv, kaug, qv, lam, b_norm_g.reshape(1, -1))


def _hgrn_chunk(q, k, v, g, st_ref, rev):
    c, sub = HGRN_CHUNK, HGRN_SUB
    nb = c // sub
    r_i = lax.broadcasted_iota(jnp.int32, (c, c), 0)
    c_i = lax.broadcasted_iota(jnp.int32, (c, c), 1)
    tri = jnp.where((c_i >= r_i) if rev else (c_i <= r_i), 1.0, 0.0).astype(F32)
    b = jnp.dot(tri, g, preferred_element_type=F32, precision=lax.Precision.HIGHEST)
    btot = b[0:1] if rev else b[c - 1:c]
    st = st_ref[...]

    qe = q * jnp.exp(b)
    o = lax.dot_general(qe.astype(BF16), st.astype(BF16), (((1,), (1,)), ((), ())), preferred_element_type=F32)

    q_parts, k_parts = [], []
    for j in (range(1, nb) if rev else range(nb - 1)):
        if rev:
            ej = b[sub * j:sub * j + 1]
            qrows = slice(0, sub * j)
        else:
            ej = b[sub * j + sub - 1:sub * j + sub]
            qrows = slice(sub * (j + 1), c)
        qp = q[qrows] * jnp.exp(b[qrows] - ej)
        kp = k[sub * j:sub * (j + 1)] * jnp.exp(ej - b[sub * j:sub * (j + 1)])
        zq = jnp.zeros((c - qp.shape[0], LANES), F32)
        q_parts.append(jnp.concatenate([qp, zq] if rev else [zq, qp], axis=0))
        pieces = []
        if j > 0:
            pieces.append(jnp.zeros((sub * j, LANES), F32))
        pieces.append(kp)
        if j < nb - 1:
            pieces.append(jnp.zeros((c - sub * (j + 1), LANES), F32))
        k_parts.append(jnp.concatenate(pieces, axis=0))
    qcat = jnp.concatenate(q_parts, axis=1).astype(BF16)
    kcat = jnp.concatenate(k_parts, axis=1).astype(BF16)
    sc = lax.dot_general(qcat, kcat, (((1,), (1,)), ((), ())), preferred_element_type=F32)
    o = o + jnp.dot(sc.astype(BF16), v.astype(BF16), preferred_element_type=F32)

    b3, q3, k3, v3 = (a.reshape(nb, sub, LANES) for a in (b, q, k, v))
    srow = lax.broadcasted_iota(jnp.int32, (1, sub, 1), 1)
    acc = jnp.zeros((nb, sub, LANES), F32)
    for j in range(sub):
        keep = (srow <= j) if rev else (srow >= j)
        w = jnp.exp(jnp.where(keep, b3 - b3[:, j:j + 1, :], NEG_BIG))
        sj = jnp.sum(q3 * w * k3[:, j:j + 1, :], axis=-1, keepdims=True)
        acc = acc + sj * v3[:, j:j + 1, :]
    o = o + acc.reshape(c, LANES)

    kd = k * jnp.exp(btot - b)
    st_ref[...] = st * jnp.exp(btot) + jnp.dot(v.T.astype(BF16), kd.astype(BF16), preferred_element_type=F32)
    return o


def _hgrn_kernel(qf_ref, if_ref, zf_ref, qb_ref, ib_ref, zb_ref, lb_ref, of_ref, ob_ref, st_sc, *, ts):
    @pl.when(pl.program_id(2) == 0)
    def _():
        st_sc[...] = jnp.zeros(st_sc.shape, F32)

    lg = lb_ref[...]
    lmax = jnp.max(lg, axis=1, keepdims=True)
    le = jnp.exp(lg - lmax)
    lb = le[:, 0, :] / jnp.sum(le, axis=1)

    nc = ts // HGRN_CHUNK

    def gates(qa, ia, z, lbd):
        f = lbd + (1.0 - lbd) * jax.nn.sigmoid(z)
        return qa * jax.nn.sigmoid(qa), 1.0 - f, ia, jnp.log(f)

    def body(ci, carry):
        off = pl.multiple_of(ci * HGRN_CHUNK, HGRN_CHUNK)
        rows = pl.ds(off, HGRN_CHUNK)
        roff = pl.multiple_of((nc - 1 - ci) * HGRN_CHUNK, HGRN_CHUNK)
        rrows = pl.ds(roff, HGRN_CHUNK)
        for hh in range(HGRN_HEADS):
            cols = slice(hh * A_HEAD_DIM, (hh + 1) * A_HEAD_DIM)
            q, k, v, g = gates(qf_ref[rows, cols], if_ref[rows, cols], zf_ref[rows, cols], lb[0:1, cols])
            of_ref[rows, cols] = _hgrn_chunk(q, k, v, g, st_sc.at[2 * hh], rev=False)
            q, k, v, g = gates(qb_ref[rrows, cols], ib_ref[rrows, cols], zb_ref[rrows, cols], lb[1:2, cols])
            ob_ref[rrows, cols] = _hgrn_chunk(q, k, v, g, st_sc.at[2 * hh + 1], rev=True)
        return carry

    lax.fori_loop(0, nc, body, 0)


def _hgrn(z_nat, lb_logits, bsz, t, ts=512):
    assert lb_logits.shape[1] == 2
    n = bsz * t
    ts = _tile(t, ts)
    ns = t // ts
    hw = HGRN_HEADS * A_HEAD_DIM
    ng = A_HEADS // HGRN_HEADS

    def fwd(col):
        return pl.BlockSpec((ts, hw), lambda b, h, i: (b * ns + i, col * ng + h))

    def bwd(col):
        return pl.BlockSpec((ts, hw), lambda b, h, i: (b * ns + ns - 1 - i, col * ng + h))

    return pl.pallas_call(
        functools.partial(_hgrn_kernel, ts=ts),
        grid=(bsz, ng, ns),
        in_specs=[fwd(0), fwd(1), fwd(2), bwd(0), bwd(1), bwd(3),
                  pl.BlockSpec((2, 2, hw), lambda b, h, i: (0, 0, h))],
        out_specs=[
            pl.BlockSpec((ts, hw), lambda b, h, i: (b * ns + i, h)),
            pl.BlockSpec((ts, hw), lambda b, h, i: (b * ns + ns - 1 - i, h)),
        ],
        out_shape=[jax.ShapeDtypeStruct((n, A_HEADS * A_HEAD_DIM), F32)] * 2,
        scratch_shapes=[pltpu.VMEM((2 * HGRN_HEADS, A_HEAD_DIM, A_HEAD_DIM), F32)],
        compiler_params=_cparams(("parallel", "parallel", "arbitrary")),
        name="hgrn2",
    )(z_nat, z_nat, z_nat, z_nat, z_nat, z_nat, lb_logits)


def _merge_kernel(of_ref, ob_ref, ga_ref, ag_ref, on_ref, gate_ref, wa_ref, wb_ref, o_ref, *, d):
    o = of_ref[...] + ob_ref[...]
    parts = []
    for h in range(A_HEADS):
        oh = o[:, h * A_HEAD_DIM:(h + 1) * A_HEAD_DIM]
        ms = jnp.mean(oh * oh, axis=-1, keepdims=True)
        parts.append(oh * lax.rsqrt(ms + EPS))
    ga = ga_ref[...]
    oa = jnp.concatenate(parts, axis=1) * ag_ref[...] * (ga * jax.nn.sigmoid(ga))
    ya = jnp.dot(oa.astype(BF16), wa_ref[...], preferred_element_type=F32)
    yb = jnp.dot(on_ref[...], wb_ref[...], preferred_element_type=F32)
    gts = jax.nn.sigmoid(gate_ref[...])
    o_ref[...] = (gts[:, :d] * ya + gts[:, d:] * yb).astype(o_ref.dtype)


def _merge(o_f, o_b, z_nat, a_norm_g, on, gate, w_a, w_b, tm=256):
    n, aw = o_f.shape
    d = w_a.shape[1]
    tm = _tile(n, tm)
    ga_col = 4 * A_HEADS * A_HEAD_DIM // aw
    return pl.pallas_call(
        functools.partial(_merge_kernel, d=d),
        grid=(n // tm,),
        in_specs=[
            pl.BlockSpec((tm, aw), lambda i: (i, 0)),
            pl.BlockSpec((tm, aw), lambda i: (i, 0)),
            pl.BlockSpec((tm, aw), lambda i: (i, ga_col)),
            pl.BlockSpec((1, aw), lambda i: (0, 0)),
            pl.BlockSpec((tm, on.shape[1]), lambda i: (i, 0)),
            pl.BlockSpec((tm, 2 * d), lambda i: (i, 0)),
            pl.BlockSpec(w_a.shape, lambda i: (0, 0)),
            pl.BlockSpec(w_b.shape, lambda i: (0, 0)),
        ],
        out_specs=pl.BlockSpec((tm, d), lambda i: (i, 0)),
        out_shape=jax.ShapeDtypeStruct((n, d), BF16),
        compiler_params=_cparams(("parallel",)),
        name="merge",
    )(o_f, o_b, z_nat, a_norm_g.reshape(1, aw), on, gate, w_a, w_b)


def _keyproj_kernel(keys_ref, wpq_ref, o_ref):
    o_ref[...] = lax.dot_general(
        keys_ref[...], wpq_ref[...], (((1,), (1,)), ((), ())),
        preferred_element_type=F32, precision=lax.Precision.HIGHEST,
    ).astype(o_ref.dtype)


def _keyproj(sub_keys, w_pq):
    d = w_pq.shape[0]
    nhc = 2 * P_HEADS
    keys = sub_keys.reshape(nhc, P_NKEYS, P_HALF)
    return pl.pallas_call(
        _keyproj_kernel,
        grid=(nhc,),
        in_specs=[
            pl.BlockSpec((None, P_NKEYS, P_HALF), lambda i: (i, 0, 0)),
            pl.BlockSpec((d, P_HALF), lambda i: (0, i)),
        ],
        out_specs=pl.BlockSpec((P_NKEYS, d), lambda i: (i, 0)),
        out_shape=jax.ShapeDtypeStruct((nhc * P_NKEYS, d), BF16),
        compiler_params=_cparams(("parallel",)),
        name="peer_keyproj",
    )(keys, w_pq)


def _topk_kernel(s_ref, n1_ref, c1_ref, r2_ref, e2_ref, *, tm):
    s1 = s_ref[0:P_NKEYS, :]
    s2 = s_ref[P_NKEYS:2 * P_NKEYS, :]

    def top_sorted(s, with_rank):
        vals = []
        work = s
        rank = jnp.full(s.shape, float(P_TOPK), F32)
        for r in range(P_TOPK):
            mx = jnp.max(work, axis=0, keepdims=True)
            vals.append(mx)
            hit = work == mx
            if with_rank:
                rank = jnp.where(hit, float(r), rank)
            work = jnp.where(hit, NEG_BIG, work)
        return vals, rank

    u, _ = top_sorted(s1, False)
    v, rank2 = top_sorted(s2, True)
    cands = [u[i] + v[j] for i in range(P_TOPK) for j in range(P_TOPK) if (i + 1) * (j + 1) <= P_TOPK]
    nrow = -(-len(cands) // SUBLANES) * SUBLANES
    crow = lax.broadcasted_iota(jnp.int32, (nrow, tm), 0)
    cand = jnp.full((nrow, tm), NEG_BIG, F32)
    for idx, cv in enumerate(cands):
        cand = jnp.where(crow == idx, cv, cand)
    top = u[0] + v[0]
    z = jnp.zeros((1, tm), F32)
    tau = top
    for _ in range(P_TOPK):
        mx = jnp.max(cand, axis=0, keepdims=True)
        z = z + jnp.exp(mx - top)
        tau = mx
        cand = jnp.where(cand == mx, NEG_BIG, cand)
    cnt = jnp.zeros(s1.shape, F32)
    for j in range(P_TOPK):
        cnt = cnt + jnp.where(s1 + v[j] >= tau, 1.0, 0.0)
    n1_ref[...] = cnt
    c1_ref[...] = jnp.exp(s1 - u[0]) / z
    r2_ref[...] = rank2.astype(r2_ref.dtype)
    e2_ref[...] = jnp.exp(s2 - v[0]).astype(e2_ref.dtype)


def _topk(st, tm=256):
    rows, n = st.shape
    tm = _tile(n, tm)
    blk = pl.BlockSpec((None, P_NKEYS, tm), lambda i, h: (h, 0, i))
    return pl.pallas_call(
        functools.partial(_topk_kernel, tm=tm),
        grid=(n // tm, P_HEADS),
        in_specs=[pl.BlockSpec((2 * P_NKEYS, tm), lambda i, h: (h, i))],
        out_specs=[blk, blk, blk, blk],
        out_shape=[
            jax.ShapeDtypeStruct((P_HEADS, P_NKEYS, n), F32),
            jax.ShapeDtypeStruct((P_HEADS, P_NKEYS, n), F32),
            jax.ShapeDtypeStruct((P_HEADS, P_NKEYS, n), BF16),
            jax.ShapeDtypeStruct((P_HEADS, P_NKEYS, n), BF16),
        ],
        compiler_params=_cparams(("parallel", "arbitrary")),
        name="peer_topk",
    )(st)


PEER_TE = SUBLANES * P_NKEYS
BF16_ROWS = 2 * SUBLANES


def _peer_kernel(xn_ref, u_ref, vt_ref, n1_ref, c1_ref, r2_ref, e2_ref, y_ref, w_sc, *, tm):
    e = pl.program_id(1)

    @pl.when(e == 0)
    def _():
        y_ref[...] = jnp.zeros(y_ref.shape, F32)

    hid = jnp.dot(u_ref[...], xn_ref[...], preferred_element_type=F32)
    for al in range(PEER_TE // P_NKEYS):
        for lc in range(tm // LANES):
            cols = slice(lc * LANES, (lc + 1) * LANES)
            cnt, c1 = [], []
            for h in range(P_HEADS):
                cnt.append(jnp.broadcast_to(n1_ref[h, al:al + 1, cols], (BF16_ROWS, LANES)).astype(BF16))
                c1.append(jnp.broadcast_to(c1_ref[h, al:al + 1, cols], (BF16_ROWS, LANES)).astype(BF16))
            for b0 in range(0, P_NKEYS, BF16_ROWS):
                brow = slice(b0, b0 + BF16_ROWS)
                acc = jnp.zeros((BF16_ROWS, LANES), BF16)
                for h in range(P_HEADS):
                    keep = r2_ref[h, brow, cols] < cnt[h]
                    acc = acc + jnp.where(keep, e2_ref[h, brow, cols] * c1[h], jnp.zeros((), BF16))
                rows = slice(al * P_NKEYS + b0, al * P_NKEYS + b0 + BF16_ROWS)
                hv = hid[rows, cols]
                gelu = 0.5 * hv * (1.0 + lax.erf(hv * (2.0 ** -0.5)))
                w_sc[rows, cols] = acc * gelu.astype(BF16)
    y_ref[...] += jnp.dot(vt_ref[...], w_sc[...], preferred_element_type=F32)


def _peer_dense(xn_t, u_bf, vt_bf, n1, c1, r2, e2, tm=512):
    d, n = xn_t.shape
    nexp = u_bf.shape[0]
    te = PEER_TE
    tm = _tile(n, tm)
    assert nexp == P_NKEYS * P_NKEYS and tm % LANES == 0
    return pl.pallas_call(
        functools.partial(_peer_kernel, tm=tm),
        grid=(n // tm, nexp // te),
        in_specs=[
            pl.BlockSpec((d, tm), lambda i, e: (0, i)),
            pl.BlockSpec((te, d), lambda i, e: (e, 0)),
            pl.BlockSpec((d, te), lambda i, e: (0, e)),
            pl.BlockSpec((P_HEADS, SUBLANES, tm), lambda i, e: (0, e, i)),
            pl.BlockSpec((P_HEADS, SUBLANES, tm), lambda i, e: (0, e, i)),
            pl.BlockSpec((P_HEADS, P_NKEYS, tm), lambda i, e: (0, 0, i)),
            pl.BlockSpec((P_HEADS, P_NKEYS, tm), lambda i, e: (0, 0, i)),
        ],
        out_specs=pl.BlockSpec((d, tm), lambda i, e: (0, i)),
        out_shape=jax.ShapeDtypeStruct((d, n), F32),
        scratch_shapes=[pltpu.VMEM((te, tm), BF16)],
        compiler_params=_cparams(("parallel", "arbitrary")),
        name="peer_dense",
    )(xn_t, u_bf, vt_bf, n1, c1, r2, e2)


def _final_kernel(x_ref, yt_ref, g_ref, o_ref):
    x = x_ref[...] + yt_ref[...].T
    ms = jnp.mean(x * x, axis=-1, keepdims=True)
    o_ref[...] = x * lax.rsqrt(ms + EPS) * g_ref[...]


def _final(x1, yt, g, tm=256):
    n, d = x1.shape
    tm = _tile(n, tm)
    return pl.pallas_call(
        _final_kernel,
        grid=(n // tm,),
        in_specs=[
            pl.BlockSpec((tm, d), lambda i: (i, 0)),
            pl.BlockSpec((d, tm), lambda i: (0, i)),
            pl.BlockSpec((1, d), lambda i: (0, 0)),
        ],
        out_specs=pl.BlockSpec((tm, d), lambda i: (i, 0)),
        out_shape=jax.ShapeDtypeStruct((n, d), F32),
        compiler_params=_cparams(("parallel",)),
        name="final_norm",
    )(x1, yt, g.reshape(1, d))


def _prepare_weights(w_in, w_a_proj, w_b_proj, w_o, w_pq, sub_keys, u_experts, v_experts):
    aw = A_HEADS * A_HEAD_DIM
    qk = 2 * B_HEADS * B_QK_DIM
    bw = B_HEADS * B_V_DIM
    w = w_in[0]
    c0 = 5 * aw
    return dict(
        w_nat=w[:, :c0].astype(BF16),
        w_qv_t=jnp.concatenate([w[:, c0:c0 + qk], w[:, c0 + 2 * qk:c0 + 2 * qk + bw]], axis=1).T.astype(BF16),
        w_k=w[:, c0 + qk:c0 + 2 * qk].astype(BF16),
        w_gate=w[:, c0 + 2 * qk + bw:].astype(BF16),
        w_a=w_a_proj[0].astype(BF16),
        w_b=w_b_proj[0].astype(BF16),
        w_o=w_o[0].astype(BF16),
        keyproj=_keyproj(sub_keys[0], w_pq[0]),
        u=u_experts[0].astype(BF16),
        vt=v_experts[0].T.astype(BF16),
    )


def _encoder(x, wts, norm1_g, lb_logits, a_norm_g, lam, b_norm_g, norm2_g, final_g):
    bsz, t, d = x.shape
    n = bsz * t
    x2 = x.reshape(n, d)

    xn = _rmsnorm(x2, norm1_g[0], BF16)
    z_nat = _matmul(xn, wts["w_nat"], F32, name="inproj_nat")
    gate = _matmul(xn, wts["w_gate"], F32, name="inproj_gate")
    qv_t = _matmul_nt_batched(wts["w_qv_t"], xn, bsz, t, BF16, name="inproj_qv_t")
    kaug = _kaug(xn, wts["w_k"], bsz, t)

    o_f, o_b = _hgrn(z_nat, lb_logits, bsz, t)
    on = _attention(qv_t, kaug, lam[0], b_norm_g[0], bsz, t)

    mixed = _merge(o_f, o_b, z_nat, a_norm_g[0], on, gate, wts["w_a"], wts["w_b"])
    x1 = _matmul(mixed, wts["w_o"], F32, residual=x2, name="out_proj")

    xn2_t = _rmsnorm_t(x1, norm2_g[0], BF16)
    st = _matmul(wts["keyproj"], xn2_t, F32, name="peer_scores")
    n1, c1, r2, e2 = _topk(st)
    yt = _peer_dense(xn2_t, wts["u"], wts["vt"], n1, c1, r2, e2)
    return _final(x1, yt, final_g).reshape(bsz, t, d)


def kernel(x_prompt, x_sample, norm1_g, w_in, lb_logits, a_norm_g, w_a_proj, lam, b_norm_g, w_b_proj, w_o, norm2_g, w_pq, sub_keys, u_experts, v_experts, final_g):
    assert w_in.shape[0] == 1
    wts = _prepare_weights(w_in, w_a_proj, w_b_proj, w_o, w_pq, sub_keys, u_experts, v_experts)
    args = (wts, norm1_g, lb_logits, a_norm_g, lam, b_norm_g, norm2_g, final_g)
    return (_encoder(x_prompt, *args), _encoder(x_sample, *args))
```

```python
import functools
import math

import jax
import jax.numpy as jnp
from jax import lax
from jax.experimental import pallas as pl
from jax.experimental.pallas import tpu as pltpu

F32 = jnp.float32
BF16 = jnp.bfloat16

A_HEADS = 8
A_HEAD_DIM = 128
B_HEADS = 8
B_QK_DIM = 64
B_V_DIM = 128
P_HEADS = 8
P_NKEYS = 128
P_HALF = 128
P_TOPK = 16
EPS = 1e-6
LAM_INIT = 0.8 - 0.6 * math.exp(-0.3 * 0)

LANES = 128
SUBLANES = 8
VMEM_LIMIT_BYTES = 56 * 1024 * 1024

NEG_BIG = -1e30
HGRN_CHUNK = 64
HGRN_SUB = SUBLANES
HGRN_HEADS = 4


def _cparams(sem, flags=None):
    return pltpu.CompilerParams(dimension_semantics=sem, vmem_limit_bytes=VMEM_LIMIT_BYTES, flags=flags)


def _tile(n, want):
    t = min(n, want)
    assert n % t == 0, (n, want)
    return t


def _rmsnorm_kernel(x_ref, g_ref, o_ref):
    x = x_ref[...]
    ms = jnp.mean(x * x, axis=-1, keepdims=True)
    o_ref[...] = (x * lax.rsqrt(ms + EPS) * g_ref[...]).astype(o_ref.dtype)


def _rmsnorm(x, g, out_dtype, tm=512):
    n, d = x.shape
    tm = _tile(n, tm)
    return pl.pallas_call(
        _rmsnorm_kernel,
        grid=(n // tm,),
        in_specs=[pl.BlockSpec((tm, d), lambda i: (i, 0)), pl.BlockSpec((1, d), lambda i: (0, 0))],
        out_specs=pl.BlockSpec((tm, d), lambda i: (i, 0)),
        out_shape=jax.ShapeDtypeStruct((n, d), out_dtype),
        compiler_params=_cparams(("parallel",)),
        name="rmsnorm",
    )(x, g.reshape(1, d))


def _rmsnorm_t_kernel(x_ref, g_ref, o_ref):
    x = x_ref[...]
    ms = jnp.mean(x * x, axis=-1, keepdims=True)
    o_ref[...] = (x * lax.rsqrt(ms + EPS) * g_ref[...]).T.astype(o_ref.dtype)


def _rmsnorm_t(x, g, out_dtype, tm=512):
    n, d = x.shape
    tm = _tile(n, tm)
    return pl.pallas_call(
        _rmsnorm_t_kernel,
        grid=(n // tm,),
        in_specs=[pl.BlockSpec((tm, d), lambda i: (i, 0)), pl.BlockSpec((1, d), lambda i: (0, 0))],
        out_specs=pl.BlockSpec((d, tm), lambda i: (0, i)),
        out_shape=jax.ShapeDtypeStruct((d, n), out_dtype),
        compiler_params=_cparams(("parallel",)),
        name="rmsnorm_t",
    )(x, g.reshape(1, d))


def _mm_kernel(a_ref, w_ref, o_ref):
    o_ref[...] = jnp.dot(a_ref[...], w_ref[...], preferred_element_type=F32).astype(o_ref.dtype)


def _mm_res_kernel(a_ref, w_ref, r_ref, o_ref):
    o_ref[...] = (r_ref[...] + jnp.dot(a_ref[...], w_ref[...], preferred_element_type=F32)).astype(o_ref.dtype)


def _matmul(a, w, out_dtype, residual=None, tm=1024, tn=1024, name="matmul"):
    n, k = a.shape
    m = w.shape[1]
    tm, tn = _tile(n, tm), _tile(m, tn)
    in_specs = [pl.BlockSpec((tm, k), lambda i, j: (i, 0)), pl.BlockSpec((k, tn), lambda i, j: (0, j))]
    args = [a, w]
    kern = _mm_kernel
    if residual is not None:
        in_specs.append(pl.BlockSpec((tm, tn), lambda i, j: (i, j)))
        args.append(residual)
        kern = _mm_res_kernel
    return pl.pallas_call(
        kern,
        grid=(n // tm, m // tn),
        in_specs=in_specs,
        out_specs=pl.BlockSpec((tm, tn), lambda i, j: (i, j)),
        out_shape=jax.ShapeDtypeStruct((n, m), out_dtype),
        compiler_params=_cparams(("parallel", "arbitrary")),
        name=name,
    )(*args)


def _mm_nt_kernel(wt_ref, a_ref, o_ref):
    o_ref[...] = lax.dot_general(
        wt_ref[...], a_ref[...], (((1,), (1,)), ((), ())), preferred_element_type=F32
    ).astype(o_ref.dtype)


def _matmul_nt_batched(wt, a, bsz, t, out_dtype, tm=1024, tn=1024, name="matmul_nt"):
    m, k = wt.shape
    tm, tn = _tile(t, tm), _tile(m, tn)
    nt = t // tm
    return pl.pallas_call(
        _mm_nt_kernel,
        grid=(bsz, nt, m // tn),
        in_specs=[
            pl.BlockSpec((tn, k), lambda b, i, j: (j, 0)),
            pl.BlockSpec((tm, k), lambda b, i, j: (b * nt + i, 0)),
        ],
        out_specs=pl.BlockSpec((None, tn, tm), lambda b, i, j: (b, j, i)),
        out_shape=jax.ShapeDtypeStruct((bsz, m, t), out_dtype),
        compiler_params=_cparams(("parallel", "parallel", "arbitrary")),
        name=name,
    )(wt, a)


def _kaug_kernel(a_ref, w_ref, o_ref, *, tm):
    i = pl.program_id(1)
    h = pl.program_id(2)
    k = jnp.dot(a_ref[...], w_ref[...], preferred_element_type=F32)
    slope = jnp.exp2(-(h + 1).astype(F32))
    pos = i * tm + lax.broadcasted_iota(jnp.int32, (tm, LANES), 0)
    lane = lax.broadcasted_iota(jnp.int32, (tm, LANES), 1)
    hi = (pos >> 6).astype(F32)
    lo = (pos & 63).astype(F32)
    aug = jnp.where(lane == 0, slope * 64.0 * hi,
                    jnp.where(lane == 1, slope * lo,
                              jnp.where(lane == 2, -64.0 * slope, jnp.where(lane == 3, -slope, 0.0))))
    o_ref[...] = jnp.concatenate([k, aug], axis=1).astype(o_ref.dtype)


def _kaug(xn, wk, bsz, t, tm=1024):
    n, d = xn.shape
    assert t <= 64 * 256
    tm = _tile(t, tm)
    nt = t // tm
    return pl.pallas_call(
        functools.partial(_kaug_kernel, tm=tm),
        grid=(bsz, nt, B_HEADS),
        in_specs=[
            pl.BlockSpec((tm, d), lambda b, i, h: (b * nt + i, 0)),
            pl.BlockSpec((d, 2 * B_QK_DIM), lambda b, i, h: (0, h)),
        ],
        out_specs=pl.BlockSpec((None, None, tm, 2 * LANES), lambda b, i, h: (b, h, i, 0)),
        out_shape=jax.ShapeDtypeStruct((bsz, B_HEADS, t, 2 * LANES), BF16),
        compiler_params=_cparams(("parallel", "parallel", "arbitrary")),
        name="kaug",
    )(xn, wk)


def _attn_kernel(qt_ref, k_ref, vt_ref, lam_ref, g_ref, o_ref, qa_sc, m_sc, l_sc, acc_sc, *, tq, nk):
    h = pl.program_id(1)
    qi = pl.program_id(2)
    slope = jnp.exp2(-(h + 1).astype(F32))

    qt = qt_ref[...].astype(F32) * (B_QK_DIM ** -0.5)
    row = lax.broadcasted_iota(jnp.int32, (LANES, tq), 0)
    ipos = qi * tq + lax.broadcasted_iota(jnp.int32, (LANES, tq), 1)
    ihi = (ipos >> 6).astype(F32)
    ilo = (ipos & 63).astype(F32)
    aug = jnp.where(row < 2, 1.0, jnp.where(row == 2, ihi, jnp.where(row == 3, ilo, 0.0)))
    for c in range(2):
        qm = jnp.where((row >= B_QK_DIM * c) & (row < B_QK_DIM * (c + 1)), qt, 0.0)
        qa_sc[c, 0] = jnp.concatenate([qm, aug], axis=0).astype(BF16)
        qa_sc[c, 1] = jnp.concatenate([qm, -aug], axis=0).astype(BF16)

    m_sc[...] = jnp.full(m_sc.shape, NEG_BIG, F32)
    l_sc[...] = jnp.zeros(l_sc.shape, F32)
    acc_sc[...] = jnp.zeros(acc_sc.shape, F32)

    def update(c, s, vt):
        m_old = m_sc[c]
        m_new = jnp.maximum(m_old, jnp.max(s, axis=0, keepdims=True))
        alpha = jnp.exp(m_old - m_new)
        p = jnp.exp(s - m_new)
        l_sc[c] = alpha * l_sc[c] + jnp.sum(p, axis=0, keepdims=True)
        acc_sc[c] = alpha * acc_sc[c] + jnp.dot(vt, p.astype(BF16), preferred_element_type=F32)
        m_sc[c] = m_new

    def side_tile(jj, carry):
        j = jj + (jj >= qi).astype(jnp.int32)
        side = (j > qi).astype(jnp.int32)
        off = pl.multiple_of(j * tq, tq)
        kt = k_ref[pl.ds(off, tq), :]
        vt = vt_ref[:, pl.ds(off, tq)]
        for c in range(2):
            s = jnp.dot(kt, qa_sc[c, side], preferred_element_type=F32)
            update(c, s, vt)
        return carry

    lax.fori_loop(0, nk - 1, side_tile, 0)

    off = pl.multiple_of(qi * tq, tq)
    kt = k_ref[pl.ds(off, tq), 0:LANES]
    vt = vt_ref[:, pl.ds(off, tq)]
    jj = lax.broadcasted_iota(jnp.int32, (tq, tq), 0)
    ii = lax.broadcasted_iota(jnp.int32, (tq, tq), 1)
    bias = -slope * jnp.abs(ii - jj).astype(F32)
    for c in range(2):
        s = jnp.dot(kt, qa_sc[c, 0, 0:LANES, :], preferred_element_type=F32) + bias
        update(c, s, vt)

    lf = lam_ref[...]
    lam_full = (jnp.exp(jnp.sum(lf[0:1] * lf[1:2], axis=-1, keepdims=True))
                - jnp.exp(jnp.sum(lf[2:3] * lf[3:4], axis=-1, keepdims=True)) + LAM_INIT)
    o = acc_sc[0] / l_sc[0] - lam_full * (acc_sc[1] / l_sc[1])
    ms = jnp.mean(o * o, axis=0, keepdims=True)
    on = o * lax.rsqrt(ms + EPS)
    on = on.T * g_ref[...] * (1.0 - LAM_INIT)
    o_ref[...] = on.astype(o_ref.dtype)


def _attention(qv_t, kaug, lam, b_norm_g, bsz, t, tq=1024):
    tq = _tile(t, tq)
    nq = t // tq
    kern = functools.partial(_attn_kernel, tq=tq, nk=nq)
    qv = qv_t.reshape(bsz * 2 * B_HEADS, LANES, t)
    return pl.pallas_call(
        kern,
        grid=(bsz, B_HEADS, nq),
        in_specs=[
            pl.BlockSpec((None, LANES, tq), lambda b, h, q: (b * 2 * B_HEADS + h, 0, q)),
            pl.BlockSpec((None, None, t, 2 * LANES), lambda b, h, q: (b, h, 0, 0)),
            pl.BlockSpec((None, LANES, t), lambda b, h, q: (b * 2 * B_HEADS + B_HEADS + h, 0, 0)),
            pl.BlockSpec((4, B_QK_DIM), lambda b, h, q: (0, 0)),
            pl.BlockSpec((1, B_V_DIM), lambda b, h, q: (0, h)),
        ],
        out_specs=pl.BlockSpec((tq, B_V_DIM), lambda b, h, q: (b * nq + q, h)),
        out_shape=jax.ShapeDtypeStruct((bsz * t, B_HEADS * B_V_DIM), BF16),
        scratch_shapes=[
            pltpu.VMEM((2, 2, 2 * LANES, tq), BF16),
            pltpu.VMEM((2, 1, tq), F32),
            pltpu.VMEM((2, 1, tq), F32),
            pltpu.VMEM((2, B_V_DIM, tq), F32),
        ],
        compiler_params=_cparams(("parallel", "parallel", "arbitrary")),
        name="diff_attention",
    )(qv, kaug, qv, lam, b_norm_g.reshape(1, -1))


def _hgrn_chunk(q, k, v, g, st_ref, rev):
    c, sub = HGRN_CHUNK, HGRN_SUB
    nb = c // sub
    r_i = lax.broadcasted_iota(jnp.int32, (c, c), 0)
    c_i = lax.broadcasted_iota(jnp.int32, (c, c), 1)
    tri = jnp.where((c_i >= r_i) if rev else (c_i <= r_i), 1.0, 0.0).astype(F32)
    b = jnp.dot(tri, g, preferred_element_type=F32, precision=lax.Precision.HIGHEST)
    btot = b[0:1] if rev else b[c - 1:c]
    st = st_ref[...]

    qe = q * jnp.exp(b)
    o = lax.dot_general(qe.astype(BF16), st.astype(BF16), (((1,), (1,)), ((), ())), preferred_element_type=F32)

    q_parts, k_parts = [], []
    for j in (range(1, nb) if rev else range(nb - 1)):
        if rev:
            ej = b[sub * j:sub * j + 1]
            qrows = slice(0, sub * j)
        else:
            ej = b[sub * j + sub - 1:sub * j + sub]
            qrows = slice(sub * (j + 1), c)
        qp = q[qrows] * jnp.exp(b[qrows] - ej)
        kp = k[sub * j:sub * (j + 1)] * jnp.exp(ej - b[sub * j:sub * (j + 1)])
        zq = jnp.zeros((c - qp.shape[0], LANES), F32)
        q_parts.append(jnp.concatenate([qp, zq] if rev else [zq, qp], axis=0))
        pieces = []
        if j > 0:
            pieces.append(jnp.zeros((sub * j, LANES), F32))
        pieces.append(kp)
        if j < nb - 1:
            pieces.append(jnp.zeros((c - sub * (j + 1), LANES), F32))
        k_parts.append(jnp.concatenate(pieces, axis=0))
    qcat = jnp.concatenate(q_parts, axis=1).astype(BF16)
    kcat = jnp.concatenate(k_parts, axis=1).astype(BF16)
    sc = lax.dot_general(qcat, kcat, (((1,), (1,)), ((), ())), preferred_element_type=F32)
    o = o + jnp.dot(sc.astype(BF16), v.astype(BF16), preferred_element_type=F32)

    b3, q3, k3, v3 = (a.reshape(nb, sub, LANES) for a in (b, q, k, v))
    srow = lax.broadcasted_iota(jnp.int32, (1, sub, 1), 1)
    acc = jnp.zeros((nb, sub, LANES), F32)
    for j in range(sub):
        keep = (srow <= j) if rev else (srow >= j)
        w = jnp.exp(jnp.where(keep, b3 - b3[:, j:j + 1, :], NEG_BIG))
        sj = jnp.sum(q3 * w * k3[:, j:j + 1, :], axis=-1, keepdims=True)
        acc = acc + sj * v3[:, j:j + 1, :]
    o = o + acc.reshape(c, LANES)

    kd = k * jnp.exp(btot - b)
    st_ref[...] = st * jnp.exp(btot) + jnp.dot(v.T.astype(BF16), kd.astype(BF16), preferred_element_type=F32)
    return o


def _hgrn_kernel(qf_ref, if_ref, zf_ref, qb_ref, ib_ref, zb_ref, lb_ref, of_ref, ob_ref, st_sc, *, ts):
    @pl.when(pl.program_id(2) == 0)
    def _():
        st_sc[...] = jnp.zeros(st_sc.shape, F32)

    lg = lb_ref[...]
    lmax = jnp.max(lg, axis=1, keepdims=True)
    le = jnp.exp(lg - lmax)
    lb = le[:, 0, :] / jnp.sum(le, axis=1)

    nc = ts // HGRN_CHUNK

    def gates(qa, ia, z, lbd):
        f = lbd + (1.0 - lbd) * jax.nn.sigmoid(z)
        return qa * jax.nn.sigmoid(qa), 1.0 - f, ia, jnp.log(f)

    def body(ci, carry):
        off = pl.multiple_of(ci * HGRN_CHUNK, HGRN_CHUNK)
        rows = pl.ds(off, HGRN_CHUNK)
        roff = pl.multiple_of((nc - 1 - ci) * HGRN_CHUNK, HGRN_CHUNK)
        rrows = pl.ds(roff, HGRN_CHUNK)
        for hh in range(HGRN_HEADS):
            cols = slice(hh * A_HEAD_DIM, (hh + 1) * A_HEAD_DIM)
            q, k, v, g = gates(qf_ref[rows, cols], if_ref[rows, cols], zf_ref[rows, cols], lb[0:1, cols])
            of_ref[rows, cols] = _hgrn_chunk(q, k, v, g, st_sc.at[2 * hh], rev=False)
            q, k, v, g = gates(qb_ref[rrows, cols], ib_ref[rrows, cols], zb_ref[rrows, cols], lb[1:2, cols])
            ob_ref[rrows, cols] = _hgrn_chunk(q, k, v, g, st_sc.at[2 * hh + 1], rev=True)
        return carry

    lax.fori_loop(0, nc, body, 0)


def _hgrn(z_nat, lb_logits, bsz, t, ts=512):
    assert lb_logits.shape[1] == 2
    n = bsz * t
    ts = _tile(t, ts)
    ns = t // ts
    hw = HGRN_HEADS * A_HEAD_DIM
    ng = A_HEADS // HGRN_HEADS

    def fwd(col):
        return pl.BlockSpec((ts, hw), lambda b, h, i: (b * ns + i, col * ng + h))

    def bwd(col):
        return pl.BlockSpec((ts, hw), lambda b, h, i: (b * ns + ns - 1 - i, col * ng + h))

    return pl.pallas_call(
        functools.partial(_hgrn_kernel, ts=ts),
        grid=(bsz, ng, ns),
        in_specs=[fwd(0), fwd(1), fwd(2), bwd(0), bwd(1), bwd(3),
                  pl.BlockSpec((2, 2, hw), lambda b, h, i: (0, 0, h))],
        out_specs=[
            pl.BlockSpec((ts, hw), lambda b, h, i: (b * ns + i, h)),
            pl.BlockSpec((ts, hw), lambda b, h, i: (b * ns + ns - 1 - i, h)),
        ],
        out_shape=[jax.ShapeDtypeStruct((n, A_HEADS * A_HEAD_DIM), F32)] * 2,
        scratch_shapes=[pltpu.VMEM((2 * HGRN_HEADS, A_HEAD_DIM, A_HEAD_DIM), F32)],
        compiler_params=_cparams(("parallel", "parallel", "arbitrary")),
        name="hgrn2",
    )(z_nat, z_nat, z_nat, z_nat, z_nat, z_nat, lb_logits)


def _merge_kernel(of_ref, ob_ref, ga_ref, ag_ref, on_ref, gate_ref, wa_ref, wb_ref, o_ref, *, d):
    o = of_ref[...] + ob_ref[...]
    parts = []
    for h in range(A_HEADS):
        oh = o[:, h * A_HEAD_DIM:(h + 1) * A_HEAD_DIM]
        ms = jnp.mean(oh * oh, axis=-1, keepdims=True)
        parts.append(oh * lax.rsqrt(ms + EPS))
    ga = ga_ref[...]
    oa = jnp.concatenate(parts, axis=1) * ag_ref[...] * (ga * jax.nn.sigmoid(ga))
    ya = jnp.dot(oa.astype(BF16), wa_ref[...], preferred_element_type=F32)
    yb = jnp.dot(on_ref[...], wb_ref[...], preferred_element_type=F32)
    gts = jax.nn.sigmoid(gate_ref[...])
    o_ref[...] = (gts[:, :d] * ya + gts[:, d:] * yb).astype(o_ref.dtype)


def _merge(o_f, o_b, z_nat, a_norm_g, on, gate, w_a, w_b, tm=256):
    n, aw = o_f.shape
    d = w_a.shape[1]
    tm = _tile(n, tm)
    ga_col = 4 * A_HEADS * A_HEAD_DIM // aw
    return pl.pallas_call(
        functools.partial(_merge_kernel, d=d),
        grid=(n // tm,),
        in_specs=[
            pl.BlockSpec((tm, aw), lambda i: (i, 0)),
            pl.BlockSpec((tm, aw), lambda i: (i, 0)),
            pl.BlockSpec((tm, aw), lambda i: (i, ga_col)),
            pl.BlockSpec((1, aw), lambda i: (0, 0)),
            pl.BlockSpec((tm, on.shape[1]), lambda i: (i, 0)),
            pl.BlockSpec((tm, 2 * d), lambda i: (i, 0)),
            pl.BlockSpec(w_a.shape, lambda i: (0, 0)),
            pl.BlockSpec(w_b.shape, lambda i: (0, 0)),
        ],
        out_specs=pl.BlockSpec((tm, d), lambda i: (i, 0)),
        out_shape=jax.ShapeDtypeStruct((n, d), BF16),
        compiler_params=_cparams(("parallel",)),
        name="merge",
    )(o_f, o_b, z_nat, a_norm_g.reshape(1, aw), on, gate, w_a, w_b)


def _keyproj_kernel(keys_ref, wpq_ref, o_ref):
    o_ref[...] = lax.dot_general(
        keys_ref[...], wpq_ref[...], (((1,), (1,)), ((), ())),
        preferred_element_type=F32, precision=lax.Precision.HIGHEST,
    ).astype(o_ref.dtype)


def _keyproj(sub_keys, w_pq):
    d = w_pq.shape[0]
    nhc = 2 * P_HEADS
    keys = sub_keys.reshape(nhc, P_NKEYS, P_HALF)
    return pl.pallas_call(
        _keyproj_kernel,
        grid=(nhc,),
        in_specs=[
            pl.BlockSpec((None, P_NKEYS, P_HALF), lambda i: (i, 0, 0)),
            pl.BlockSpec((d, P_HALF), lambda i: (0, i)),
        ],
        out_specs=pl.BlockSpec((P_NKEYS, d), lambda i: (i, 0)),
        out_shape=jax.ShapeDtypeStruct((nhc * P_NKEYS, d), BF16),
        compiler_params=_cparams(("parallel",)),
        name="peer_keyproj",
    )(keys, w_pq)


def _topk_kernel(s_ref, n1_ref, c1_ref, r2_ref, e2_ref, *, tm):
    s1 = s_ref[0:P_NKEYS, :]
    s2 = s_ref[P_NKEYS:2 * P_NKEYS, :]

    def top_sorted(s, with_rank):
        vals = []
        work = s
        rank = jnp.full(s.shape, float(P_TOPK), F32)
        for r in range(P_TOPK):
            mx = jnp.max(work, axis=0, keepdims=True)
            vals.append(mx)
            hit = work == mx
            if with_rank:
                rank = jnp.where(hit, float(r), rank)
            work = jnp.where(hit, NEG_BIG, work)
        return vals, rank

    u, _ = top_sorted(s1, False)
    v, rank2 = top_sorted(s2, True)
    cands = [u[i] + v[j] for i in range(P_TOPK) for j in range(P_TOPK) if (i + 1) * (j + 1) <= P_TOPK]
    nrow = -(-len(cands) // SUBLANES) * SUBLANES
    crow = lax.broadcasted_iota(jnp.int32, (nrow, tm), 0)
    cand = jnp.full((nrow, tm), NEG_BIG, F32)
    for idx, cv in enumerate(cands):
        cand = jnp.where(crow == idx, cv, cand)
    top = u[0] + v[0]
    z = jnp.zeros((1, tm), F32)
    tau = top
    for _ in range(P_TOPK):
        mx = jnp.max(cand, axis=0, keepdims=True)
        z = z + jnp.exp(mx - top)
        tau = mx
        cand = jnp.where(cand == mx, NEG_BIG, cand)
    cnt = jnp.zeros(s1.shape, F32)
    for j in range(P_TOPK):
        cnt = jnp.where(s1 + v[j] >= tau, float(j + 1), cnt)
    n1_ref[...] = cnt
    c1_ref[...] = jnp.exp(s1 - u[0]) / z
    r2_ref[...] = pltpu.bitcast(rank2.astype(BF16), jnp.uint32)
    e2_ref[...] = pltpu.bitcast(jnp.exp(s2 - v[0]).astype(BF16), jnp.uint32)


def _topk(st, tm=256):
    rows, n = st.shape
    tm = _tile(n, tm)
    blk = pl.BlockSpec((None, P_NKEYS, tm), lambda i, h: (h, 0, i))
    pblk = pl.BlockSpec((None, P_NKEYS // 2, tm), lambda i, h: (h, 0, i))
    return pl.pallas_call(
        functools.partial(_topk_kernel, tm=tm),
        grid=(n // tm, P_HEADS),
        in_specs=[pl.BlockSpec((2 * P_NKEYS, tm), lambda i, h: (h, i))],
        out_specs=[blk, blk, pblk, pblk],
        out_shape=[
            jax.ShapeDtypeStruct((P_HEADS, P_NKEYS, n), F32),
            jax.ShapeDtypeStruct((P_HEADS, P_NKEYS, n), F32),
            jax.ShapeDtypeStruct((P_HEADS, P_NKEYS // 2, n), jnp.uint32),
            jax.ShapeDtypeStruct((P_HEADS, P_NKEYS // 2, n), jnp.uint32),
        ],
        compiler_params=_cparams(("parallel", "arbitrary")),
        name="peer_topk",
    )(st)


PEER_TE = SUBLANES * P_NKEYS
BF16_ROWS = 2 * SUBLANES


def _peer_kernel(xn_ref, u_ref, vt_ref, n1_ref, c1_ref, r2_ref, e2_ref, y_ref, w_sc, *, tm):
    e = pl.program_id(1)

    @pl.when(e == 0)
    def _():
        y_ref[...] = jnp.zeros(y_ref.shape, F32)

    hid = jnp.dot(u_ref[...], xn_ref[...], preferred_element_type=F32)
    for al in range(PEER_TE // P_NKEYS):
        for lc in range(tm // LANES):
            cols = slice(lc * LANES, (lc + 1) * LANES)
            cnt, c1 = [], []
            for h in range(P_HEADS):
                cnt.append(jnp.broadcast_to(n1_ref[h, al:al + 1, cols], (BF16_ROWS, LANES)).astype(BF16))
                c1.append(jnp.broadcast_to(c1_ref[h, al:al + 1, cols], (BF16_ROWS, LANES)).astype(BF16))
            for b0 in range(0, P_NKEYS, BF16_ROWS):
                prow = slice(b0 // 2, (b0 + BF16_ROWS) // 2)
                acc = jnp.zeros((BF16_ROWS, LANES), BF16)
                for h in range(P_HEADS):
                    keep = pltpu.bitcast(r2_ref[h, prow, cols], BF16) < cnt[h]
                    gate = pltpu.bitcast(e2_ref[h, prow, cols], BF16) * c1[h]
                    acc = acc + jnp.where(keep, gate, jnp.zeros((), BF16))
                rows = slice(al * P_NKEYS + b0, al * P_NKEYS + b0 + BF16_ROWS)
                hv = hid[rows, cols]
                gelu = 0.5 * hv * (1.0 + lax.erf(hv * (2.0 ** -0.5)))
                w_sc[(al * P_NKEYS + b0) // 2:(al * P_NKEYS + b0 + BF16_ROWS) // 2, cols] = pltpu.bitcast(
                    acc * gelu.astype(BF16), jnp.uint32)
    y_ref[...] += jnp.dot(vt_ref[...], pltpu.bitcast(w_sc[...], BF16), preferred_element_type=F32)


def _peer_dense(xn_t, u_bf, vt_bf, n1, c1, r2, e2, tm=512):
    d, n = xn_t.shape
    nexp = u_bf.shape[0]
    te = PEER_TE
    tm = _tile(n, tm)
    assert nexp == P_NKEYS * P_NKEYS and tm % LANES == 0
    return pl.pallas_call(
        functools.partial(_peer_kernel, tm=tm),
        grid=(n // tm, nexp // te),
        in_specs=[
            pl.BlockSpec((d, tm), lambda i, e: (0, i)),
            pl.BlockSpec((te, d), lambda i, e: (e, 0)),
            pl.BlockSpec((d, te), lambda i, e: (0, e)),
            pl.BlockSpec((P_HEADS, SUBLANES, tm), lambda i, e: (0, e, i)),
            pl.BlockSpec((P_HEADS, SUBLANES, tm), lambda i, e: (0, e, i)),
            pl.BlockSpec((P_HEADS, P_NKEYS // 2, tm), lambda i, e: (0, 0, i)),
            pl.BlockSpec((P_HEADS, P_NKEYS // 2, tm), lambda i, e: (0, 0, i)),
        ],
        out_specs=pl.BlockSpec((d, tm), lambda i, e: (0, i)),
        out_shape=jax.ShapeDtypeStruct((d, n), F32),
        scratch_shapes=[pltpu.VMEM((te // 2, tm), jnp.uint32)],
        compiler_params=_cparams(("parallel", "arbitrary")),
        name="peer_dense",
    )(xn_t, u_bf, vt_bf, n1, c1, r2, e2)


def _final_kernel(x_ref, yt_ref, g_ref, o_ref):
    x = x_ref[...] + yt_ref[...].T
    ms = jnp.mean(x * x, axis=-1, keepdims=True)
    o_ref[...] = x * lax.rsqrt(ms + EPS) * g_ref[...]


def _final(x1, yt, g, tm=256):
    n, d = x1.shape
    tm = _tile(n, tm)
    return pl.pallas_call(
        _final_kernel,
        grid=(n // tm,),
        in_specs=[
            pl.BlockSpec((tm, d), lambda i: (i, 0)),
            pl.BlockSpec((d, tm), lambda i: (0, i)),
            pl.BlockSpec((1, d), lambda i: (0, 0)),
        ],
        out_specs=pl.BlockSpec((tm, d), lambda i: (i, 0)),
        out_shape=jax.ShapeDtypeStruct((n, d), F32),
        compiler_params=_cparams(("parallel",)),
        name="final_norm",
    )(x1, yt, g.reshape(1, d))


def _prepare_weights(w_in, w_a_proj, w_b_proj, w_o, w_pq, sub_keys, u_experts, v_experts):
    aw = A_HEADS * A_HEAD_DIM
    qk = 2 * B_HEADS * B_QK_DIM
    bw = B_HEADS * B_V_DIM
    w = w_in[0]
    c0 = 5 * aw
    return dict(
        w_nat=w[:, :c0].astype(BF16),
        w_qv_t=jnp.concatenate([w[:, c0:c0 + qk], w[:, c0 + 2 * qk:c0 + 2 * qk + bw]], axis=1).T.astype(BF16),
        w_k=w[:, c0 + qk:c0 + 2 * qk].astype(BF16),
        w_gate=w[:, c0 + 2 * qk + bw:].astype(BF16),
        w_a=w_a_proj[0].astype(BF16),
        w_b=w_b_proj[0].astype(BF16),
        w_o=w_o[0].astype(BF16),
        keyproj=_keyproj(sub_keys[0], w_pq[0]),
        u=u_experts[0].astype(BF16),
        vt=v_experts[0].T.astype(BF16),
    )


def _encoder(x, wts, norm1_g, lb_logits, a_norm_g, lam, b_norm_g, norm2_g, final_g):
    bsz, t, d = x.shape
    n = bsz * t
    x2 = x.reshape(n, d)

    xn = _rmsnorm(x2, norm1_g[0], BF16)
    z_nat = _matmul(xn, wts["w_nat"], F32, name="inproj_nat")
    gate = _matmul(xn, wts["w_gate"], F32, name="inproj_gate")
    qv_t = _matmul_nt_batched(wts["w_qv_t"], xn, bsz, t, BF16, name="inproj_qv_t")
    kaug = _kaug(xn, wts["w_k"], bsz, t)

    o_f, o_b = _hgrn(z_nat, lb_logits, bsz, t)
    on = _attention(qv_t, kaug, lam[0], b_norm_g[0], bsz, t)

    mixed = _merge(o_f, o_b, z_nat, a_norm_g[0], on, gate, wts["w_a"], wts["w_b"])
    x1 = _matmul(mixed, wts["w_o"], F32, residual=x2, name="out_proj")

    xn2_t = _rmsnorm_t(x1, norm2_g[0], BF16)
    st = _matmul(wts["keyproj"], xn2_t, F32, name="peer_scores")
    n1, c1, r2, e2 = _topk(st)
    yt = _peer_dense(xn2_t, wts["u"], wts["vt"], n1, c1, r2, e2)
    return _final(x1, yt, final_g).reshape(bsz, t, d)


def kernel(x_prompt, x_sample, norm1_g, w_in, lb_logits, a_norm_g, w_a_proj, lam, b_norm_g, w_b_proj, w_o, norm2_g, w_pq, sub_keys, u_experts, v_experts, final_g):
    assert w_in.shape[0] == 1
    wts = _prepare_weights(w_in, w_a_proj, w_b_proj, w_o, w_pq, sub_keys, u_experts, v_experts)
    args = (wts, norm1_g, lb_logits, a_norm_g, lam, b_norm_g, norm2_g, final_g)
    return (_encoder(x_prompt, *args), _encoder(x_sample, *args))
```

```python
import functools
import math

import jax
import jax.numpy as jnp
from jax import lax
from jax.experimental import pallas as pl
from jax.experimental.pallas import tpu as pltpu

F32 = jnp.float32
BF16 = jnp.bfloat16

A_HEADS = 8
A_HEAD_DIM = 128
B_HEADS = 8
B_QK_DIM = 64
B_V_DIM = 128
P_HEADS = 8
P_NKEYS = 128
P_HALF = 128
P_TOPK = 16
EPS = 1e-6
LAM_INIT = 0.8 - 0.6 * math.exp(-0.3 * 0)

LANES = 128
SUBLANES = 8
VMEM_LIMIT_BYTES = 56 * 1024 * 1024

NEG_BIG = -1e30
HGRN_CHUNK = 64
HGRN_SUB = SUBLANES
HGRN_HEADS = 4
KAUG_HEADS = 2


def _cparams(sem, flags=None):
    return pltpu.CompilerParams(dimension_semantics=sem, vmem_limit_bytes=VMEM_LIMIT_BYTES, flags=flags)


def _tile(n, want):
    t = min(n, want)
    assert n % t == 0, (n, want)
    return t


def _rmsnorm_kernel(x_ref, g_ref, o_ref):
    x = x_ref[...]
    ms = jnp.mean(x * x, axis=-1, keepdims=True)
    o_ref[...] = (x * lax.rsqrt(ms + EPS) * g_ref[...]).astype(o_ref.dtype)


def _rmsnorm(x, g, out_dtype, tm=512):
    n, d = x.shape
    tm = _tile(n, tm)
    return pl.pallas_call(
        _rmsnorm_kernel,
        grid=(n // tm,),
        in_specs=[pl.BlockSpec((tm, d), lambda i: (i, 0)), pl.BlockSpec((1, d), lambda i: (0, 0))],
        out_specs=pl.BlockSpec((tm, d), lambda i: (i, 0)),
        out_shape=jax.ShapeDtypeStruct((n, d), out_dtype),
        compiler_params=_cparams(("parallel",)),
        name="rmsnorm",
    )(x, g.reshape(1, d))


def _rmsnorm_t_kernel(x_ref, g_ref, o_ref):
    x = x_ref[...]
    ms = jnp.mean(x * x, axis=-1, keepdims=True)
    o_ref[...] = (x * lax.rsqrt(ms + EPS) * g_ref[...]).T.astype(o_ref.dtype)


def _rmsnorm_t(x, g, out_dtype, tm=512):
    n, d = x.shape
    tm = _tile(n, tm)
    return pl.pallas_call(
        _rmsnorm_t_kernel,
        grid=(n // tm,),
        in_specs=[pl.BlockSpec((tm, d), lambda i: (i, 0)), pl.BlockSpec((1, d), lambda i: (0, 0))],
        out_specs=pl.BlockSpec((d, tm), lambda i: (0, i)),
        out_shape=jax.ShapeDtypeStruct((d, n), out_dtype),
        compiler_params=_cparams(("parallel",)),
        name="rmsnorm_t",
    )(x, g.reshape(1, d))


def _mm_kernel(a_ref, w_ref, o_ref):
    o_ref[...] = jnp.dot(a_ref[...], w_ref[...], preferred_element_type=F32).astype(o_ref.dtype)


def _mm_res_kernel(a_ref, w_ref, r_ref, o_ref):
    o_ref[...] = (r_ref[...] + jnp.dot(a_ref[...], w_ref[...], preferred_element_type=F32)).astype(o_ref.dtype)


def _matmul(a, w, out_dtype, residual=None, tm=1024, tn=1024, name="matmul"):
    n, k = a.shape
    m = w.shape[1]
    tm, tn = _tile(n, tm), _tile(m, tn)
    in_specs = [pl.BlockSpec((tm, k), lambda i, j: (i, 0)), pl.BlockSpec((k, tn), lambda i, j: (0, j))]
    args = [a, w]
    kern = _mm_kernel
    if residual is not None:
        in_specs.append(pl.BlockSpec((tm, tn), lambda i, j: (i, j)))
        args.append(residual)
        kern = _mm_res_kernel
    return pl.pallas_call(
        kern,
        grid=(n // tm, m // tn),
        in_specs=in_specs,
        out_specs=pl.BlockSpec((tm, tn), lambda i, j: (i, j)),
        out_shape=jax.ShapeDtypeStruct((n, m), out_dtype),
        compiler_params=_cparams(("parallel", "arbitrary")),
        name=name,
    )(*args)


def _mm_nt_kernel(wt_ref, a_ref, o_ref):
    o_ref[...] = lax.dot_general(
        wt_ref[...], a_ref[...], (((1,), (1,)), ((), ())), preferred_element_type=F32
    ).astype(o_ref.dtype)


def _matmul_nt_batched(wt, a, bsz, t, out_dtype, tm=1024, tn=1024, name="matmul_nt"):
    m, k = wt.shape
    tm, tn = _tile(t, tm), _tile(m, tn)
    nt = t // tm
    return pl.pallas_call(
        _mm_nt_kernel,
        grid=(bsz, nt, m // tn),
        in_specs=[
            pl.BlockSpec((tn, k), lambda b, i, j: (j, 0)),
            pl.BlockSpec((tm, k), lambda b, i, j: (b * nt + i, 0)),
        ],
        out_specs=pl.BlockSpec((None, tn, tm), lambda b, i, j: (b, j, i)),
        out_shape=jax.ShapeDtypeStruct((bsz, m, t), out_dtype),
        compiler_params=_cparams(("parallel", "parallel", "arbitrary")),
        name=name,
    )(wt, a)


def _kaug_kernel(a_ref, w_ref, o_ref, *, tm):
    i = pl.program_id(1)
    hp = pl.program_id(2)
    k = jnp.dot(a_ref[...], w_ref[...], preferred_element_type=F32)
    pos = i * tm + lax.broadcasted_iota(jnp.int32, (tm, LANES), 0)
    lane = lax.broadcasted_iota(jnp.int32, (tm, LANES), 1)
    hi = (pos >> 6).astype(F32)
    lo = (pos & 63).astype(F32)
    for hh in range(KAUG_HEADS):
        slope = jnp.exp2(-(hp * KAUG_HEADS + hh + 1).astype(F32))
        aug = jnp.where(lane == 0, slope * 64.0 * hi,
                        jnp.where(lane == 1, slope * lo,
                                  jnp.where(lane == 2, -64.0 * slope, jnp.where(lane == 3, -slope, 0.0))))
        o_ref[hh] = jnp.concatenate([k[:, hh * LANES:(hh + 1) * LANES], aug], axis=1).astype(o_ref.dtype)


def _kaug(xn, wk, bsz, t, tm=1024):
    n, d = xn.shape
    assert t <= 64 * 256
    tm = _tile(t, tm)
    nt = t // tm
    return pl.pallas_call(
        functools.partial(_kaug_kernel, tm=tm),
        grid=(bsz, nt, B_HEADS // KAUG_HEADS),
        in_specs=[
            pl.BlockSpec((tm, d), lambda b, i, h: (b * nt + i, 0)),
            pl.BlockSpec((d, KAUG_HEADS * 2 * B_QK_DIM), lambda b, i, h: (0, h)),
        ],
        out_specs=pl.BlockSpec((None, KAUG_HEADS, tm, 2 * LANES), lambda b, i, h: (b, h, i, 0)),
        out_shape=jax.ShapeDtypeStruct((bsz, B_HEADS, t, 2 * LANES), BF16),
        compiler_params=_cparams(("parallel", "parallel", "arbitrary")),
        name="kaug",
    )(xn, wk)


def _attn_kernel(qt_ref, k_ref, vt_ref, lam_ref, g_ref, o_ref, qa_sc, m_sc, l_sc, acc_sc, *, tq, nk):
    h = pl.program_id(1)
    qi = pl.program_id(2)
    slope = jnp.exp2(-(h + 1).astype(F32))

    qt = qt_ref[...].astype(F32) * (B_QK_DIM ** -0.5)
    row = lax.broadcasted_iota(jnp.int32, (LANES, tq), 0)
    ipos = qi * tq + lax.broadcasted_iota(jnp.int32, (LANES, tq), 1)
    ihi = (ipos >> 6).astype(F32)
    ilo = (ipos & 63).astype(F32)
    aug = jnp.where(row < 2, 1.0, jnp.where(row == 2, ihi, jnp.where(row == 3, ilo, 0.0)))
    for c in range(2):
        qm = jnp.where((row >= B_QK_DIM * c) & (row < B_QK_DIM * (c + 1)), qt, 0.0)
        qa_sc[c, 0] = jnp.concatenate([qm, aug], axis=0).astype(BF16)
        qa_sc[c, 1] = jnp.concatenate([qm, -aug], axis=0).astype(BF16)

    m_sc[...] = jnp.full(m_sc.shape, NEG_BIG, F32)
    l_sc[...] = jnp.zeros(l_sc.shape, F32)
    acc_sc[...] = jnp.zeros(acc_sc.shape, F32)

    def update(c, s, vt):
        m_old = m_sc[c]
        m_new = jnp.maximum(m_old, jnp.max(s, axis=0, keepdims=True))
        alpha = jnp.exp(m_old - m_new)
        p = jnp.exp(s - m_new)
        l_sc[c] = alpha * l_sc[c] + jnp.sum(p, axis=0, keepdims=True)
        acc_sc[c] = alpha * acc_sc[c] + jnp.dot(vt, p.astype(BF16), preferred_element_type=F32)
        m_sc[c] = m_new

    def side_tile(jj, carry):
        j = jj + (jj >= qi).astype(jnp.int32)
        side = (j > qi).astype(jnp.int32)
        off = pl.multiple_of(j * tq, tq)
        kt = k_ref[pl.ds(off, tq), :]
        vt = vt_ref[:, pl.ds(off, tq)]
        scores = [jnp.dot(kt, qa_sc[c, side], preferred_element_type=F32) for c in range(2)]
        for c in range(2):
            update(c, scores[c], vt)
        return carry

    lax.fori_loop(0, nk - 1, side_tile, 0)

    off = pl.multiple_of(qi * tq, tq)
    kt = k_ref[pl.ds(off, tq), 0:LANES]
    vt = vt_ref[:, pl.ds(off, tq)]
    jj = lax.broadcasted_iota(jnp.int32, (tq, tq), 0)
    ii = lax.broadcasted_iota(jnp.int32, (tq, tq), 1)
    bias = -slope * jnp.abs(ii - jj).astype(F32)
    for c in range(2):
        s = jnp.dot(kt, qa_sc[c, 0, 0:LANES, :], preferred_element_type=F32) + bias
        update(c, s, vt)

    lf = lam_ref[...]
    lam_full = (jnp.exp(jnp.sum(lf[0:1] * lf[1:2], axis=-1, keepdims=True))
                - jnp.exp(jnp.sum(lf[2:3] * lf[3:4], axis=-1, keepdims=True)) + LAM_INIT)
    o = acc_sc[0] / l_sc[0] - lam_full * (acc_sc[1] / l_sc[1])
    ms = jnp.mean(o * o, axis=0, keepdims=True)
    on = o * lax.rsqrt(ms + EPS)
    on = on.T * g_ref[...] * (1.0 - LAM_INIT)
    o_ref[...] = on.astype(o_ref.dtype)


def _attention(qv_t, kaug, lam, b_norm_g, bsz, t, tq=1024):
    tq = _tile(t, tq)
    nq = t // tq
    kern = functools.partial(_attn_kernel, tq=tq, nk=nq)
    qv = qv_t.reshape(bsz * 2 * B_HEADS, LANES, t)
    return pl.pallas_call(
        kern,
        grid=(bsz, B_HEADS, nq),
        in_specs=[
            pl.BlockSpec((None, LANES, tq), lambda b, h, q: (b * 2 * B_HEADS + h, 0, q)),
            pl.BlockSpec((None, None, t, 2 * LANES), lambda b, h, q: (b, h, 0, 0)),
            pl.BlockSpec((None, LANES, t), lambda b, h, q: (b * 2 * B_HEADS + B_HEADS + h, 0, 0)),
            pl.BlockSpec((4, B_QK_DIM), lambda b, h, q: (0, 0)),
            pl.BlockSpec((1, B_V_DIM), lambda b, h, q: (0, h)),
        ],
        out_specs=pl.BlockSpec((tq, B_V_DIM), lambda b, h, q: (b * nq + q, h)),
        out_shape=jax.ShapeDtypeStruct((bsz * t, B_HEADS * B_V_DIM), BF16),
        scratch_shapes=[
            pltpu.VMEM((2, 2, 2 * LANES, tq), BF16),
            pltpu.VMEM((2, 1, tq), F32),
            pltpu.VMEM((2, 1, tq), F32),
            pltpu.VMEM((2, B_V_DIM, tq), F32),
        ],
        compiler_params=_cparams(("parallel", "parallel", "arbitrary")),
        name="diff_attention",
    )(qv, kaug, qv, lam, b_norm_g.reshape(1, -1))


def _hgrn_chunk(q, k, v, g, st_ref, rev):
    c, sub = HGRN_CHUNK, HGRN_SUB
    nb = c // sub
    r_i = lax.broadcasted_iota(jnp.int32, (c, c), 0)
    c_i = lax.broadcasted_iota(jnp.int32, (c, c), 1)
    tri = jnp.where((c_i >= r_i) if rev else (c_i <= r_i), 1.0, 0.0).astype(F32)
    b = jnp.dot(tri, g, preferred_element_type=F32, precision=lax.Precision.HIGHEST)
    btot = b[0:1] if rev else b[c - 1:c]
    st = st_ref[...]

    qe = q * jnp.exp(b)
    o = lax.dot_general(qe.astype(BF16), st.astype(BF16), (((1,), (1,)), ((), ())), preferred_element_type=F32)

    q_parts, k_parts = [], []
    for j in (range(1, nb) if rev else range(nb - 1)):
        if rev:
            ej = b[sub * j:sub * j + 1]
            qrows = slice(0, sub * j)
        else:
            ej = b[sub * j + sub - 1:sub * j + sub]
            qrows = slice(sub * (j + 1), c)
        qp = q[qrows] * jnp.exp(b[qrows] - ej)
        kp = k[sub * j:sub * (j + 1)] * jnp.exp(ej - b[sub * j:sub * (j + 1)])
        zq = jnp.zeros((c - qp.shape[0], LANES), F32)
        q_parts.append(jnp.concatenate([qp, zq] if rev else [zq, qp], axis=0))
        pieces = []
        if j > 0:
            pieces.append(jnp.zeros((sub * j, LANES), F32))
        pieces.append(kp)
        if j < nb - 1:
            pieces.append(jnp.zeros((c - sub * (j + 1), LANES), F32))
        k_parts.append(jnp.concatenate(pieces, axis=0))
    qcat = jnp.concatenate(q_parts, axis=1).astype(BF16)
    kcat = jnp.concatenate(k_parts, axis=1).astype(BF16)
    sc = lax.dot_general(qcat, kcat, (((1,), (1,)), ((), ())), preferred_element_type=F32)
    o = o + jnp.dot(sc.astype(BF16), v.astype(BF16), preferred_element_type=F32)

    b3, q3, k3, v3 = (a.reshape(nb, sub, LANES) for a in (b, q, k, v))
    srow = lax.broadcasted_iota(jnp.int32, (1, sub, 1), 1)
    acc = jnp.zeros((nb, sub, LANES), F32)
    for j in range(sub):
        keep = (srow <= j) if rev else (srow >= j)
        w = jnp.exp(jnp.where(keep, b3 - b3[:, j:j + 1, :], NEG_BIG))
        sj = jnp.sum(q3 * w * k3[:, j:j + 1, :], axis=-1, keepdims=True)
        acc = acc + sj * v3[:, j:j + 1, :]
    o = o + acc.reshape(c, LANES)

    kd = k * jnp.exp(btot - b)
    st_ref[...] = st * jnp.exp(btot) + jnp.dot(v.T.astype(BF16), kd.astype(BF16), preferred_element_type=F32)
    return o


def _hgrn_kernel(qf_ref, if_ref, zf_ref, qb_ref, ib_ref, zb_ref, lb_ref, of_ref, ob_ref, st_sc, *, ts):
    @pl.when(pl.program_id(2) == 0)
    def _():
        st_sc[...] = jnp.zeros(st_sc.shape, F32)

    lg = lb_ref[...]
    lmax = jnp.max(lg, axis=1, keepdims=True)
    le = jnp.exp(lg - lmax)
    lb = le[:, 0, :] / jnp.sum(le, axis=1)

    nc = ts // HGRN_CHUNK

    def gates(qa, ia, z, lbd):
        f = lbd + (1.0 - lbd) * jax.nn.sigmoid(z)
        return qa * jax.nn.sigmoid(qa), 1.0 - f, ia, jnp.log(f)

    def body(ci, carry):
        off = pl.multiple_of(ci * HGRN_CHUNK, HGRN_CHUNK)
        rows = pl.ds(off, HGRN_CHUNK)
        roff = pl.multiple_of((nc - 1 - ci) * HGRN_CHUNK, HGRN_CHUNK)
        rrows = pl.ds(roff, HGRN_CHUNK)
        for hh in range(HGRN_HEADS):
            cols = slice(hh * A_HEAD_DIM, (hh + 1) * A_HEAD_DIM)
            q, k, v, g = gates(qf_ref[rows, cols], if_ref[rows, cols], zf_ref[rows, cols], lb[0:1, cols])
            of_ref[rows, cols] = _hgrn_chunk(q, k, v, g, st_sc.at[2 * hh], rev=False)
            q, k, v, g = gates(qb_ref[rrows, cols], ib_ref[rrows, cols], zb_ref[rrows, cols], lb[1:2, cols])
            ob_ref[rrows, cols] = _hgrn_chunk(q, k, v, g, st_sc.at[2 * hh + 1], rev=True)
        return carry

    lax.fori_loop(0, nc, body, 0)


def _hgrn(z_nat, lb_logits, bsz, t, ts=512):
    assert lb_logits.shape[1] == 2
    n = bsz * t
    ts = _tile(t, ts)
    ns = t // ts
    hw = HGRN_HEADS * A_HEAD_DIM
    ng = A_HEADS // HGRN_HEADS

    def fwd(col):
        return pl.BlockSpec((ts, hw), lambda b, h, i: (b * ns + i, col * ng + h))

    def bwd(col):
        return pl.BlockSpec((ts, hw), lambda b, h, i: (b * ns + ns - 1 - i, col * ng + h))

    return pl.pallas_call(
        functools.partial(_hgrn_kernel, ts=ts),
        grid=(bsz, ng, ns),
        in_specs=[fwd(0), fwd(1), fwd(2), bwd(0), bwd(1), bwd(3),
                  pl.BlockSpec((2, 2, hw), lambda b, h, i: (0, 0, h))],
        out_specs=[
            pl.BlockSpec((ts, hw), lambda b, h, i: (b * ns + i, h)),
            pl.BlockSpec((ts, hw), lambda b, h, i: (b * ns + ns - 1 - i, h)),
        ],
        out_shape=[jax.ShapeDtypeStruct((n, A_HEADS * A_HEAD_DIM), F32)] * 2,
        scratch_shapes=[pltpu.VMEM((2 * HGRN_HEADS, A_HEAD_DIM, A_HEAD_DIM), F32)],
        compiler_params=_cparams(("parallel", "parallel", "arbitrary")),
        name="hgrn2",
    )(z_nat, z_nat, z_nat, z_nat, z_nat, z_nat, lb_logits)


def _merge_kernel(of_ref, ob_ref, ga_ref, ag_ref, on_ref, gate_ref, wa_ref, wb_ref, o_ref, *, d):
    o = of_ref[...] + ob_ref[...]
    parts = []
    for h in range(A_HEADS):
        oh = o[:, h * A_HEAD_DIM:(h + 1) * A_HEAD_DIM]
        ms = jnp.mean(oh * oh, axis=-1, keepdims=True)
        parts.append(oh * lax.rsqrt(ms + EPS))
    ga = ga_ref[...]
    oa = jnp.concatenate(parts, axis=1) * ag_ref[...] * (ga * jax.nn.sigmoid(ga))
    ya = jnp.dot(oa.astype(BF16), wa_ref[...], preferred_element_type=F32)
    yb = jnp.dot(on_ref[...], wb_ref[...], preferred_element_type=F32)
    gts = jax.nn.sigmoid(gate_ref[...])
    o_ref[...] = (gts[:, :d] * ya + gts[:, d:] * yb).astype(o_ref.dtype)


def _merge(o_f, o_b, z_nat, a_norm_g, on, gate, w_a, w_b, tm=256):
    n, aw = o_f.shape
    d = w_a.shape[1]
    tm = _tile(n, tm)
    ga_col = 4 * A_HEADS * A_HEAD_DIM // aw
    return pl.pallas_call(
        functools.partial(_merge_kernel, d=d),
        grid=(n // tm,),
        in_specs=[
            pl.BlockSpec((tm, aw), lambda i: (i, 0)),
            pl.BlockSpec((tm, aw), lambda i: (i, 0)),
            pl.BlockSpec((tm, aw), lambda i: (i, ga_col)),
            pl.BlockSpec((1, aw), lambda i: (0, 0)),
            pl.BlockSpec((tm, on.shape[1]), lambda i: (i, 0)),
            pl.BlockSpec((tm, 2 * d), lambda i: (i, 0)),
            pl.BlockSpec(w_a.shape, lambda i: (0, 0)),
            pl.BlockSpec(w_b.shape, lambda i: (0, 0)),
        ],
        out_specs=pl.BlockSpec((tm, d), lambda i: (i, 0)),
        out_shape=jax.ShapeDtypeStruct((n, d), BF16),
        compiler_params=_cparams(("parallel",)),
        name="merge",
    )(o_f, o_b, z_nat, a_norm_g.reshape(1, aw), on, gate, w_a, w_b)


def _keyproj_kernel(keys_ref, wpq_ref, o_ref):
    o_ref[...] = lax.dot_general(
        keys_ref[...], wpq_ref[...], (((1,), (1,)), ((), ())),
        preferred_element_type=F32, precision=lax.Precision.HIGHEST,
    ).astype(o_ref.dtype)


def _keyproj(sub_keys, w_pq):
    d = w_pq.shape[0]
    nhc = 2 * P_HEADS
    keys = sub_keys.reshape(nhc, P_NKEYS, P_HALF)
    return pl.pallas_call(
        _keyproj_kernel,
        grid=(nhc,),
        in_specs=[
            pl.BlockSpec((None, P_NKEYS, P_HALF), lambda i: (i, 0, 0)),
            pl.BlockSpec((d, P_HALF), lambda i: (0, i)),
        ],
        out_specs=pl.BlockSpec((P_NKEYS, d), lambda i: (i, 0)),
        out_shape=jax.ShapeDtypeStruct((nhc * P_NKEYS, d), BF16),
        compiler_params=_cparams(("parallel",)),
        name="peer_keyproj",
    )(keys, w_pq)


def _topk_kernel(s_ref, n1_ref, c1_ref, r2_ref, e2_ref, *, tm):
    s1 = s_ref[0:P_NKEYS, :]
    s2 = s_ref[P_NKEYS:2 * P_NKEYS, :]

    def top_sorted(s, with_rank):
        vals = []
        work = s
        rank = jnp.full(s.shape, float(P_TOPK), F32)
        for r in range(P_TOPK):
            mx = jnp.max(work, axis=0, keepdims=True)
            vals.append(mx)
            hit = work == mx
            if with_rank:
                rank = jnp.where(hit, float(r), rank)
            work = jnp.where(hit, NEG_BIG, work)
        return vals, rank

    u, _ = top_sorted(s1, False)
    v, rank2 = top_sorted(s2, True)
    cands = [u[i] + v[j] for i in range(P_TOPK) for j in range(P_TOPK) if (i + 1) * (j + 1) <= P_TOPK]
    nrow = -(-len(cands) // SUBLANES) * SUBLANES
    crow = lax.broadcasted_iota(jnp.int32, (nrow, tm), 0)
    cand = jnp.full((nrow, tm), NEG_BIG, F32)
    for idx, cv in enumerate(cands):
        cand = jnp.where(crow == idx, cv, cand)
    top = u[0] + v[0]
    z = jnp.zeros((1, tm), F32)
    tau = top
    for _ in range(P_TOPK):
        mx = jnp.max(cand, axis=0, keepdims=True)
        z = z + jnp.exp(mx - top)
        tau = mx
        cand = jnp.where(cand == mx, NEG_BIG, cand)
    cnt = jnp.zeros(s1.shape, F32)
    for j in range(P_TOPK):
        cnt = jnp.where(s1 + v[j] >= tau, float(j + 1), cnt)
    n1_ref[...] = cnt
    c1_ref[...] = jnp.exp(s1 - u[0]) / z
    r2_ref[...] = pltpu.bitcast(rank2.astype(BF16), jnp.uint32)
    e2_ref[...] = pltpu.bitcast(jnp.exp(s2 - v[0]).astype(BF16), jnp.uint32)


def _topk(st, tm=256):
    rows, n = st.shape
    tm = _tile(n, tm)
    blk = pl.BlockSpec((None, P_NKEYS, tm), lambda i, h: (h, 0, i))
    pblk = pl.BlockSpec((None, P_NKEYS // 2, tm), lambda i, h: (h, 0, i))
    return pl.pallas_call(
        functools.partial(_topk_kernel, tm=tm),
        grid=(n // tm, P_HEADS),
        in_specs=[pl.BlockSpec((2 * P_NKEYS, tm), lambda i, h: (h, i))],
        out_specs=[blk, blk, pblk, pblk],
        out_shape=[
            jax.ShapeDtypeStruct((P_HEADS, P_NKEYS, n), F32),
            jax.ShapeDtypeStruct((P_HEADS, P_NKEYS, n), F32),
            jax.ShapeDtypeStruct((P_HEADS, P_NKEYS // 2, n), jnp.uint32),
            jax.ShapeDtypeStruct((P_HEADS, P_NKEYS // 2, n), jnp.uint32),
        ],
        compiler_params=_cparams(("parallel", "arbitrary")),
        name="peer_topk",
    )(st)


PEER_TE = SUBLANES * P_NKEYS
BF16_ROWS = 2 * SUBLANES


def _peer_kernel(xn_ref, u_ref, vt_ref, n1_ref, c1_ref, r2_ref, e2_ref, y_ref, w_sc, *, tm):
    e = pl.program_id(1)

    @pl.when(e == 0)
    def _():
        y_ref[...] = jnp.zeros(y_ref.shape, F32)

    hid = jnp.dot(u_ref[...], xn_ref[...], preferred_element_type=F32)
    for al in range(PEER_TE // P_NKEYS):
        for lc in range(tm // LANES):
            cols = slice(lc * LANES, (lc + 1) * LANES)
            cnt, c1 = [], []
            for h in range(P_HEADS):
                cnt.append(jnp.broadcast_to(n1_ref[h, al:al + 1, cols], (BF16_ROWS, LANES)).astype(BF16))
                c1.append(jnp.broadcast_to(c1_ref[h, al:al + 1, cols], (BF16_ROWS, LANES)).astype(BF16))
            for b0 in range(0, P_NKEYS, BF16_ROWS):
                prow = slice(b0 // 2, (b0 + BF16_ROWS) // 2)
                acc = jnp.zeros((BF16_ROWS, LANES), BF16)
                for h in range(P_HEADS):
                    keep = pltpu.bitcast(r2_ref[h, prow, cols], BF16) < cnt[h]
                    gate = pltpu.bitcast(e2_ref[h, prow, cols], BF16) * c1[h]
                    acc = acc + jnp.where(keep, gate, jnp.zeros((), BF16))
                rows = slice(al * P_NKEYS + b0, al * P_NKEYS + b0 + BF16_ROWS)
                hv = hid[rows, cols]
                gelu = 0.5 * hv * (1.0 + lax.erf(hv * (2.0 ** -0.5)))
                w_sc[(al * P_NKEYS + b0) // 2:(al * P_NKEYS + b0 + BF16_ROWS) // 2, cols] = pltpu.bitcast(
                    acc * gelu.astype(BF16), jnp.uint32)
    y_ref[...] += jnp.dot(vt_ref[...], pltpu.bitcast(w_sc[...], BF16), preferred_element_type=F32)


def _peer_dense(xn_t, u_bf, vt_bf, n1, c1, r2, e2, tm=512):
    d, n = xn_t.shape
    nexp = u_bf.shape[0]
    te = PEER_TE
    tm = _tile(n, tm)
    assert nexp == P_NKEYS * P_NKEYS and tm % LANES == 0
    return pl.pallas_call(
        functools.partial(_peer_kernel, tm=tm),
        grid=(n // tm, nexp // te),
        in_specs=[
            pl.BlockSpec((d, tm), lambda i, e: (0, i)),
            pl.BlockSpec((te, d), lambda i, e: (e, 0)),
            pl.BlockSpec((d, te), lambda i, e: (0, e)),
            pl.BlockSpec((P_HEADS, SUBLANES, tm), lambda i, e: (0, e, i)),
            pl.BlockSpec((P_HEADS, SUBLANES, tm), lambda i, e: (0, e, i)),
            pl.BlockSpec((P_HEADS, P_NKEYS // 2, tm), lambda i, e: (0, 0, i)),
            pl.BlockSpec((P_HEADS, P_NKEYS // 2, tm), lambda i, e: (0, 0, i)),
        ],
        out_specs=pl.BlockSpec((d, tm), lambda i, e: (0, i)),
        out_shape=jax.ShapeDtypeStruct((d, n), F32),
        scratch_shapes=[pltpu.VMEM((te // 2, tm), jnp.uint32)],
        compiler_params=_cparams(("parallel", "arbitrary")),
        name="peer_dense",
    )(xn_t, u_bf, vt_bf, n1, c1, r2, e2)


def _final_kernel(x_ref, yt_ref, g_ref, o_ref):
    x = x_ref[...] + yt_ref[...].T
    ms = jnp.mean(x * x, axis=-1, keepdims=True)
    o_ref[...] = x * lax.rsqrt(ms + EPS) * g_ref[...]


def _final(x1, yt, g, tm=256):
    n, d = x1.shape
    tm = _tile(n, tm)
    return pl.pallas_call(
        _final_kernel,
        grid=(n // tm,),
        in_specs=[
            pl.BlockSpec((tm, d), lambda i: (i, 0)),
            pl.BlockSpec((d, tm), lambda i: (0, i)),
            pl.BlockSpec((1, d), lambda i: (0, 0)),
        ],
        out_specs=pl.BlockSpec((tm, d), lambda i: (i, 0)),
        out_shape=jax.ShapeDtypeStruct((n, d), F32),
        compiler_params=_cparams(("parallel",)),
        name="final_norm",
    )(x1, yt, g.reshape(1, d))


def _prepare_weights(w_in, w_a_proj, w_b_proj, w_o, w_pq, sub_keys, u_experts, v_experts):
    aw = A_HEADS * A_HEAD_DIM
    qk = 2 * B_HEADS * B_QK_DIM
    bw = B_HEADS * B_V_DIM
    w = w_in[0]
    c0 = 5 * aw
    return dict(
        w_nat=w[:, :c0].astype(BF16),
        w_qv_t=jnp.concatenate([w[:, c0:c0 + qk], w[:, c0 + 2 * qk:c0 + 2 * qk + bw]], axis=1).T.astype(BF16),
        w_k=w[:, c0 + qk:c0 + 2 * qk].astype(BF16),
        w_gate=w[:, c0 + 2 * qk + bw:].astype(BF16),
        w_a=w_a_proj[0].astype(BF16),
        w_b=w_b_proj[0].astype(BF16),
        w_o=w_o[0].astype(BF16),
        keyproj=_keyproj(sub_keys[0], w_pq[0]),
        u=u_experts[0].astype(BF16),
        vt=v_experts[0].T.astype(BF16),
    )


def _encoder(x, wts, norm1_g, lb_logits, a_norm_g, lam, b_norm_g, norm2_g, final_g):
    bsz, t, d = x.shape
    n = bsz * t
    x2 = x.reshape(n, d)

    xn = _rmsnorm(x2, norm1_g[0], BF16)
    z_nat = _matmul(xn, wts["w_nat"], F32, name="inproj_nat")
    gate = _matmul(xn, wts["w_gate"], F32, name="inproj_gate")
    qv_t = _matmul_nt_batched(wts["w_qv_t"], xn, bsz, t, BF16, name="inproj_qv_t")
    kaug = _kaug(xn, wts["w_k"], bsz, t)

    o_f, o_b = _hgrn(z_nat, lb_logits, bsz, t)
    on = _attention(qv_t, kaug, lam[0], b_norm_g[0], bsz, t)

    mixed = _merge(o_f, o_b, z_nat, a_norm_g[0], on, gate, wts["w_a"], wts["w_b"])
    x1 = _matmul(mixed, wts["w_o"], F32, residual=x2, name="out_proj")

    xn2_t = _rmsnorm_t(x1, norm2_g[0], BF16)
    st = _matmul(wts["keyproj"], xn2_t, F32, name="peer_scores")
    n1, c1, r2, e2 = _topk(st)
    yt = _peer_dense(xn2_t, wts["u"], wts["vt"], n1, c1, r2, e2)
    return _final(x1, yt, final_g).reshape(bsz, t, d)


def kernel(x_prompt, x_sample, norm1_g, w_in, lb_logits, a_norm_g, w_a_proj, lam, b_norm_g, w_b_proj, w_o, norm2_g, w_pq, sub_keys, u_experts, v_experts, final_g):
    assert w_in.shape[0] == 1
    wts = _prepare_weights(w_in, w_a_proj, w_b_proj, w_o, w_pq, sub_keys, u_experts, v_experts)
    args = (wts, norm1_g, lb_logits, a_norm_g, lam, b_norm_g, norm2_g, final_g)
    return (_encoder(x_prompt, *args), _encoder(x_sample, *args))
```

```python
import functools
import math

import jax
import jax.numpy as jnp
from jax import lax
from jax.experimental import pallas as pl
from jax.experimental.pallas import tpu as pltpu

F32 = jnp.float32
BF16 = jnp.bfloat16

A_HEADS = 8
A_HEAD_DIM = 128
B_HEADS = 8
B_QK_DIM = 64
B_V_DIM = 128
P_HEADS = 8
P_NKEYS = 128
P_HALF = 128
P_TOPK = 16
EPS = 1e-6
LAM_INIT = 0.8 - 0.6 * math.exp(-0.3 * 0)

LANES = 128
SUBLANES = 8
VMEM_LIMIT_BYTES = 56 * 1024 * 1024

NEG_BIG = -1e30
HGRN_CHUNK = 64
HGRN_SUB = SUBLANES
HGRN_HEADS = 8
KAUG_HEADS = 2


def _cparams(sem, flags=None):
    return pltpu.CompilerParams(dimension_semantics=sem, vmem_limit_bytes=VMEM_LIMIT_BYTES, flags=flags)


def _tile(n, want):
    t = min(n, want)
    assert n % t == 0, (n, want)
    return t


def _rmsnorm_kernel(x_ref, g_ref, o_ref):
    x = x_ref[...]
    ms = jnp.mean(x * x, axis=-1, keepdims=True)
    o_ref[...] = (x * lax.rsqrt(ms + EPS) * g_ref[...]).astype(o_ref.dtype)


def _rmsnorm(x, g, out_dtype, tm=512):
    n, d = x.shape
    tm = _tile(n, tm)
    return pl.pallas_call(
        _rmsnorm_kernel,
        grid=(n // tm,),
        in_specs=[pl.BlockSpec((tm, d), lambda i: (i, 0)), pl.BlockSpec((1, d), lambda i: (0, 0))],
        out_specs=pl.BlockSpec((tm, d), lambda i: (i, 0)),
        out_shape=jax.ShapeDtypeStruct((n, d), out_dtype),
        compiler_params=_cparams(("parallel",)),
        name="rmsnorm",
    )(x, g.reshape(1, d))


def _rmsnorm_t_kernel(x_ref, g_ref, o_ref):
    x = x_ref[...]
    ms = jnp.mean(x * x, axis=-1, keepdims=True)
    o_ref[...] = (x * lax.rsqrt(ms + EPS) * g_ref[...]).T.astype(o_ref.dtype)


def _rmsnorm_t(x, g, out_dtype, tm=512):
    n, d = x.shape
    tm = _tile(n, tm)
    return pl.pallas_call(
        _rmsnorm_t_kernel,
        grid=(n // tm,),
        in_specs=[pl.BlockSpec((tm, d), lambda i: (i, 0)), pl.BlockSpec((1, d), lambda i: (0, 0))],
        out_specs=pl.BlockSpec((d, tm), lambda i: (0, i)),
        out_shape=jax.ShapeDtypeStruct((d, n), out_dtype),
        compiler_params=_cparams(("parallel",)),
        name="rmsnorm_t",
    )(x, g.reshape(1, d))


def _mm_kernel(a_ref, w_ref, o_ref):
    o_ref[...] = jnp.dot(a_ref[...], w_ref[...], preferred_element_type=F32).astype(o_ref.dtype)


def _mm_res_kernel(a_ref, w_ref, r_ref, o_ref):
    o_ref[...] = (r_ref[...] + jnp.dot(a_ref[...], w_ref[...], preferred_element_type=F32)).astype(o_ref.dtype)


def _matmul(a, w, out_dtype, residual=None, tm=1024, tn=1024, name="matmul"):
    n, k = a.shape
    m = w.shape[1]
    tm, tn = _tile(n, tm), _tile(m, tn)
    in_specs = [pl.BlockSpec((tm, k), lambda i, j: (i, 0)), pl.BlockSpec((k, tn), lambda i, j: (0, j))]
    args = [a, w]
    kern = _mm_kernel
    if residual is not None:
        in_specs.append(pl.BlockSpec((tm, tn), lambda i, j: (i, j)))
        args.append(residual)
        kern = _mm_res_kernel
    return pl.pallas_call(
        kern,
        grid=(n // tm, m // tn),
        in_specs=in_specs,
        out_specs=pl.BlockSpec((tm, tn), lambda i, j: (i, j)),
        out_shape=jax.ShapeDtypeStruct((n, m), out_dtype),
        compiler_params=_cparams(("parallel", "arbitrary")),
        name=name,
    )(*args)


def _mm_nt_kernel(wt_ref, a_ref, o_ref):
    o_ref[...] = lax.dot_general(
        wt_ref[...], a_ref[...], (((1,), (1,)), ((), ())), preferred_element_type=F32
    ).astype(o_ref.dtype)


def _matmul_nt_batched(wt, a, bsz, t, out_dtype, tm=1024, tn=1024, name="matmul_nt"):
    m, k = wt.shape
    tm, tn = _tile(t, tm), _tile(m, tn)
    nt = t // tm
    return pl.pallas_call(
        _mm_nt_kernel,
        grid=(bsz, nt, m // tn),
        in_specs=[
            pl.BlockSpec((tn, k), lambda b, i, j: (j, 0)),
            pl.BlockSpec((tm, k), lambda b, i, j: (b * nt + i, 0)),
        ],
        out_specs=pl.BlockSpec((None, tn, tm), lambda b, i, j: (b, j, i)),
        out_shape=jax.ShapeDtypeStruct((bsz, m, t), out_dtype),
        compiler_params=_cparams(("parallel", "parallel", "arbitrary")),
        name=name,
    )(wt, a)


def _kaug_kernel(a_ref, w_ref, o_ref, *, tm):
    i = pl.program_id(1)
    hp = pl.program_id(2)
    k = jnp.dot(a_ref[...], w_ref[...], preferred_element_type=F32)
    pos = i * tm + lax.broadcasted_iota(jnp.int32, (tm, LANES), 0)
    lane = lax.broadcasted_iota(jnp.int32, (tm, LANES), 1)
    hi = (pos >> 6).astype(F32)
    lo = (pos & 63).astype(F32)
    for hh in range(KAUG_HEADS):
        slope = jnp.exp2(-(hp * KAUG_HEADS + hh + 1).astype(F32))
        aug = jnp.where(lane == 0, slope * 64.0 * hi,
                        jnp.where(lane == 1, slope * lo,
                                  jnp.where(lane == 2, -64.0 * slope, jnp.where(lane == 3, -slope, 0.0))))
        o_ref[hh] = jnp.concatenate([k[:, hh * LANES:(hh + 1) * LANES], aug], axis=1).astype(o_ref.dtype)


def _kaug(xn, wk, bsz, t, tm=1024):
    n, d = xn.shape
    assert t <= 64 * 256
    tm = _tile(t, tm)
    nt = t // tm
    return pl.pallas_call(
        functools.partial(_kaug_kernel, tm=tm),
        grid=(bsz, nt, B_HEADS // KAUG_HEADS),
        in_specs=[
            pl.BlockSpec((tm, d), lambda b, i, h: (b * nt + i, 0)),
            pl.BlockSpec((d, KAUG_HEADS * 2 * B_QK_DIM), lambda b, i, h: (0, h)),
        ],
        out_specs=pl.BlockSpec((None, KAUG_HEADS, tm, 2 * LANES), lambda b, i, h: (b, h, i, 0)),
        out_shape=jax.ShapeDtypeStruct((bsz, B_HEADS, t, 2 * LANES), BF16),
        compiler_params=_cparams(("parallel", "parallel", "arbitrary")),
        name="kaug",
    )(xn, wk)


def _attn_kernel(qt_ref, k_ref, vt_ref, lam_ref, g_ref, o_ref, qa_sc, m_sc, l_sc, acc_sc, *, tq, nk):
    h = pl.program_id(1)
    qi = pl.program_id(2)
    slope = jnp.exp2(-(h + 1).astype(F32))

    qt = qt_ref[...].astype(F32) * (B_QK_DIM ** -0.5)
    row = lax.broadcasted_iota(jnp.int32, (LANES, tq), 0)
    ipos = qi * tq + lax.broadcasted_iota(jnp.int32, (LANES, tq), 1)
    ihi = (ipos >> 6).astype(F32)
    ilo = (ipos & 63).astype(F32)
    aug = jnp.where(row < 2, 1.0, jnp.where(row == 2, ihi, jnp.where(row == 3, ilo, 0.0)))
    for c in range(2):
        qm = jnp.where((row >= B_QK_DIM * c) & (row < B_QK_DIM * (c + 1)), qt, 0.0)
        qa_sc[c, 0] = jnp.concatenate([qm, aug], axis=0).astype(BF16)
        qa_sc[c, 1] = jnp.concatenate([qm, -aug], axis=0).astype(BF16)

    m_sc[...] = jnp.full(m_sc.shape, NEG_BIG, F32)
    l_sc[...] = jnp.zeros(l_sc.shape, F32)
    acc_sc[...] = jnp.zeros(acc_sc.shape, F32)

    def update(c, s, vt):
        m_old = m_sc[c]
        m_new = jnp.maximum(m_old, jnp.max(s, axis=0, keepdims=True))
        alpha = jnp.exp(m_old - m_new)
        p = jnp.exp(s - m_new)
        l_sc[c] = alpha * l_sc[c] + jnp.sum(p, axis=0, keepdims=True)
        acc_sc[c] = alpha * acc_sc[c] + jnp.dot(vt, p.astype(BF16), preferred_element_type=F32)
        m_sc[c] = m_new

    def side_tile(jj, carry):
        j = jj + (jj >= qi).astype(jnp.int32)
        side = (j > qi).astype(jnp.int32)
        off = pl.multiple_of(j * tq, tq)
        kt = k_ref[pl.ds(off, tq), :]
        vt = vt_ref[:, pl.ds(off, tq)]
        scores = [jnp.dot(kt, qa_sc[c, side], preferred_element_type=F32) for c in range(2)]
        for c in range(2):
            update(c, scores[c], vt)
        return carry

    lax.fori_loop(0, nk - 1, side_tile, 0)

    off = pl.multiple_of(qi * tq, tq)
    kt = k_ref[pl.ds(off, tq), 0:LANES]
    vt = vt_ref[:, pl.ds(off, tq)]
    jj = lax.broadcasted_iota(jnp.int32, (tq, tq), 0)
    ii = lax.broadcasted_iota(jnp.int32, (tq, tq), 1)
    bias = -slope * jnp.abs(ii - jj).astype(F32)
    for c in range(2):
        s = jnp.dot(kt, qa_sc[c, 0, 0:LANES, :], preferred_element_type=F32) + bias
        update(c, s, vt)

    lf = lam_ref[...]
    lam_full = (jnp.exp(jnp.sum(lf[0:1] * lf[1:2], axis=-1, keepdims=True))
                - jnp.exp(jnp.sum(lf[2:3] * lf[3:4], axis=-1, keepdims=True)) + LAM_INIT)
    o = acc_sc[0] / l_sc[0] - lam_full * (acc_sc[1] / l_sc[1])
    ms = jnp.mean(o * o, axis=0, keepdims=True)
    on = o * lax.rsqrt(ms + EPS)
    on = on.T * g_ref[...] * (1.0 - LAM_INIT)
    o_ref[...] = on.astype(o_ref.dtype)


def _attention(qv_t, kaug, lam, b_norm_g, bsz, t, tq=1024):
    tq = _tile(t, tq)
    nq = t // tq
    kern = functools.partial(_attn_kernel, tq=tq, nk=nq)
    qv = qv_t.reshape(bsz * 2 * B_HEADS, LANES, t)
    return pl.pallas_call(
        kern,
        grid=(bsz, B_HEADS, nq),
        in_specs=[
            pl.BlockSpec((None, LANES, tq), lambda b, h, q: (b * 2 * B_HEADS + h, 0, q)),
            pl.BlockSpec((None, None, t, 2 * LANES), lambda b, h, q: (b, h, 0, 0)),
            pl.BlockSpec((None, LANES, t), lambda b, h, q: (b * 2 * B_HEADS + B_HEADS + h, 0, 0)),
            pl.BlockSpec((4, B_QK_DIM), lambda b, h, q: (0, 0)),
            pl.BlockSpec((1, B_V_DIM), lambda b, h, q: (0, h)),
        ],
        out_specs=pl.BlockSpec((tq, B_V_DIM), lambda b, h, q: (b * nq + q, h)),
        out_shape=jax.ShapeDtypeStruct((bsz * t, B_HEADS * B_V_DIM), BF16),
        scratch_shapes=[
            pltpu.VMEM((2, 2, 2 * LANES, tq), BF16),
            pltpu.VMEM((2, 1, tq), F32),
            pltpu.VMEM((2, 1, tq), F32),
            pltpu.VMEM((2, B_V_DIM, tq), F32),
        ],
        compiler_params=_cparams(("parallel", "parallel", "arbitrary")),
        name="diff_attention",
    )(qv, kaug, qv, lam, b_norm_g.reshape(1, -1))


def _hgrn_chunk(q, k, v, g, st_ref, rev):
    c, sub = HGRN_CHUNK, HGRN_SUB
    nb = c // sub
    r_i = lax.broadcasted_iota(jnp.int32, (c, c), 0)
    c_i = lax.broadcasted_iota(jnp.int32, (c, c), 1)
    tri = jnp.where((c_i >= r_i) if rev else (c_i <= r_i), 1.0, 0.0).astype(F32)
    b = jnp.dot(tri, g, preferred_element_type=F32, precision=lax.Precision.HIGHEST)
    btot = b[0:1] if rev else b[c - 1:c]
    st = st_ref[...]

    qe = q * jnp.exp(b)
    o = lax.dot_general(qe.astype(BF16), st.astype(BF16), (((1,), (1,)), ((), ())), preferred_element_type=F32)

    q_parts, k_parts = [], []
    for j in (range(1, nb) if rev else range(nb - 1)):
        if rev:
            ej = b[sub * j:sub * j + 1]
            qrows = slice(0, sub * j)
        else:
            ej = b[sub * j + sub - 1:sub * j + sub]
            qrows = slice(sub * (j + 1), c)
        qp = q[qrows] * jnp.exp(b[qrows] - ej)
        kp = k[sub * j:sub * (j + 1)] * jnp.exp(ej - b[sub * j:sub * (j + 1)])
        zq = jnp.zeros((c - qp.shape[0], LANES), F32)
        q_parts.append(jnp.concatenate([qp, zq] if rev else [zq, qp], axis=0))
        pieces = []
        if j > 0:
            pieces.append(jnp.zeros((sub * j, LANES), F32))
        pieces.append(kp)
        if j < nb - 1:
            pieces.append(jnp.zeros((c - sub * (j + 1), LANES), F32))
        k_parts.append(jnp.concatenate(pieces, axis=0))
    qcat = jnp.concatenate(q_parts, axis=1).astype(BF16)
    kcat = jnp.concatenate(k_parts, axis=1).astype(BF16)
    sc = lax.dot_general(qcat, kcat, (((1,), (1,)), ((), ())), preferred_element_type=F32)
    o = o + jnp.dot(sc.astype(BF16), v.astype(BF16), preferred_element_type=F32)

    b3, q3, k3, v3 = (a.reshape(nb, sub, LANES) for a in (b, q, k, v))
    srow = lax.broadcasted_iota(jnp.int32, (1, sub, 1), 1)
    acc = jnp.zeros((nb, sub, LANES), F32)
    for j in range(sub):
        keep = (srow <= j) if rev else (srow >= j)
        w = jnp.exp(jnp.where(keep, b3 - b3[:, j:j + 1, :], NEG_BIG))
        sj = jnp.sum(q3 * w * k3[:, j:j + 1, :], axis=-1, keepdims=True)
        acc = acc + sj * v3[:, j:j + 1, :]
    o = o + acc.reshape(c, LANES)

    kd = k * jnp.exp(btot - b)
    st_ref[...] = st * jnp.exp(btot) + jnp.dot(v.T.astype(BF16), kd.astype(BF16), preferred_element_type=F32)
    return o


def _hgrn_kernel(qf_ref, if_ref, zf_ref, qb_ref, ib_ref, zb_ref, lb_ref, of_ref, ob_ref, st_sc, *, ts):
    @pl.when(pl.program_id(2) == 0)
    def _():
        st_sc[...] = jnp.zeros(st_sc.shape, F32)

    lg = lb_ref[...]
    lmax = jnp.max(lg, axis=1, keepdims=True)
    le = jnp.exp(lg - lmax)
    lb = le[:, 0, :] / jnp.sum(le, axis=1)

    nc = ts // HGRN_CHUNK

    def gates(qa, ia, z, lbd):
        f = lbd + (1.0 - lbd) * jax.nn.sigmoid(z)
        return qa * jax.nn.sigmoid(qa), 1.0 - f, ia, jnp.log(f)

    def body(ci, carry):
        off = pl.multiple_of(ci * HGRN_CHUNK, HGRN_CHUNK)
        rows = pl.ds(off, HGRN_CHUNK)
        roff = pl.multiple_of((nc - 1 - ci) * HGRN_CHUNK, HGRN_CHUNK)
        rrows = pl.ds(roff, HGRN_CHUNK)
        for hh in range(HGRN_HEADS):
            cols = slice(hh * A_HEAD_DIM, (hh + 1) * A_HEAD_DIM)
            q, k, v, g = gates(qf_ref[rows, cols], if_ref[rows, cols], zf_ref[rows, cols], lb[0:1, cols])
            of_ref[rows, cols] = _hgrn_chunk(q, k, v, g, st_sc.at[2 * hh], rev=False)
            q, k, v, g = gates(qb_ref[rrows, cols], ib_ref[rrows, cols], zb_ref[rrows, cols], lb[1:2, cols])
            ob_ref[rrows, cols] = _hgrn_chunk(q, k, v, g, st_sc.at[2 * hh + 1], rev=True)
        return carry

    lax.fori_loop(0, nc, body, 0)


def _hgrn(z_nat, lb_logits, bsz, t, ts=512):
    assert lb_logits.shape[1] == 2
    n = bsz * t
    ts = _tile(t, ts)
    ns = t // ts
    hw = HGRN_HEADS * A_HEAD_DIM
    ng = A_HEADS // HGRN_HEADS

    def fwd(col):
        return pl.BlockSpec((ts, hw), lambda b, h, i: (b * ns + i, col * ng + h))

    def bwd(col):
        return pl.BlockSpec((ts, hw), lambda b, h, i: (b * ns + ns - 1 - i, col * ng + h))

    return pl.pallas_call(
        functools.partial(_hgrn_kernel, ts=ts),
        grid=(bsz, ng, ns),
        in_specs=[fwd(0), fwd(1), fwd(2), bwd(0), bwd(1), bwd(3),
                  pl.BlockSpec((2, 2, hw), lambda b, h, i: (0, 0, h))],
        out_specs=[
            pl.BlockSpec((ts, hw), lambda b, h, i: (b * ns + i, h)),
            pl.BlockSpec((ts, hw), lambda b, h, i: (b * ns + ns - 1 - i, h)),
        ],
        out_shape=[jax.ShapeDtypeStruct((n, A_HEADS * A_HEAD_DIM), F32)] * 2,
        scratch_shapes=[pltpu.VMEM((2 * HGRN_HEADS, A_HEAD_DIM, A_HEAD_DIM), F32)],
        compiler_params=_cparams(("parallel", "parallel", "arbitrary")),
        name="hgrn2",
    )(z_nat, z_nat, z_nat, z_nat, z_nat, z_nat, lb_logits)


def _merge_kernel(of_ref, ob_ref, ga_ref, ag_ref, on_ref, gate_ref, wa_ref, wb_ref, o_ref, *, d):
    o = of_ref[...] + ob_ref[...]
    parts = []
    for h in range(A_HEADS):
        oh = o[:, h * A_HEAD_DIM:(h + 1) * A_HEAD_DIM]
        ms = jnp.mean(oh * oh, axis=-1, keepdims=True)
        parts.append(oh * lax.rsqrt(ms + EPS))
    ga = ga_ref[...]
    oa = jnp.concatenate(parts, axis=1) * ag_ref[...] * (ga * jax.nn.sigmoid(ga))
    ya = jnp.dot(oa.astype(BF16), wa_ref[...], preferred_element_type=F32)
    yb = jnp.dot(on_ref[...], wb_ref[...], preferred_element_type=F32)
    gts = jax.nn.sigmoid(gate_ref[...])
    o_ref[...] = (gts[:, :d] * ya + gts[:, d:] * yb).astype(o_ref.dtype)


def _merge(o_f, o_b, z_nat, a_norm_g, on, gate, w_a, w_b, tm=256):
    n, aw = o_f.shape
    d = w_a.shape[1]
    tm = _tile(n, tm)
    ga_col = 4 * A_HEADS * A_HEAD_DIM // aw
    return pl.pallas_call(
        functools.partial(_merge_kernel, d=d),
        grid=(n // tm,),
        in_specs=[
            pl.BlockSpec((tm, aw), lambda i: (i, 0)),
            pl.BlockSpec((tm, aw), lambda i: (i, 0)),
            pl.BlockSpec((tm, aw), lambda i: (i, ga_col)),
            pl.BlockSpec((1, aw), lambda i: (0, 0)),
            pl.BlockSpec((tm, on.shape[1]), lambda i: (i, 0)),
            pl.BlockSpec((tm, 2 * d), lambda i: (i, 0)),
            pl.BlockSpec(w_a.shape, lambda i: (0, 0)),
            pl.BlockSpec(w_b.shape, lambda i: (0, 0)),
        ],
        out_specs=pl.BlockSpec((tm, d), lambda i: (i, 0)),
        out_shape=jax.ShapeDtypeStruct((n, d), BF16),
        compiler_params=_cparams(("parallel",)),
        name="merge",
    )(o_f, o_b, z_nat, a_norm_g.reshape(1, aw), on, gate, w_a, w_b)


def _keyproj_kernel(keys_ref, wpq_ref, o_ref):
    o_ref[...] = lax.dot_general(
        keys_ref[...], wpq_ref[...], (((1,), (1,)), ((), ())),
        preferred_element_type=F32, precision=lax.Precision.HIGHEST,
    ).astype(o_ref.dtype)


def _keyproj(sub_keys, w_pq):
    d = w_pq.shape[0]
    nhc = 2 * P_HEADS
    keys = sub_keys.reshape(nhc, P_NKEYS, P_HALF)
    return pl.pallas_call(
        _keyproj_kernel,
        grid=(nhc,),
        in_specs=[
            pl.BlockSpec((None, P_NKEYS, P_HALF), lambda i: (i, 0, 0)),
            pl.BlockSpec((d, P_HALF), lambda i: (0, i)),
        ],
        out_specs=pl.BlockSpec((P_NKEYS, d), lambda i: (i, 0)),
        out_shape=jax.ShapeDtypeStruct((nhc * P_NKEYS, d), BF16),
        compiler_params=_cparams(("parallel",)),
        name="peer_keyproj",
    )(keys, w_pq)


def _topk_kernel(s_ref, n1_ref, c1_ref, r2_ref, e2_ref, *, tm):
    s1 = s_ref[0:P_NKEYS, :]
    s2 = s_ref[P_NKEYS:2 * P_NKEYS, :]

    def top_sorted(s, with_rank):
        vals = []
        work = s
        rank = jnp.full(s.shape, float(P_TOPK), F32)
        for r in range(P_TOPK):
            mx = jnp.max(work, axis=0, keepdims=True)
            vals.append(mx)
            hit = work == mx
            if with_rank:
                rank = jnp.where(hit, float(r), rank)
            work = jnp.where(hit, NEG_BIG, work)
        return vals, rank

    u, _ = top_sorted(s1, False)
    v, rank2 = top_sorted(s2, True)
    cands = [u[i] + v[j] for i in range(P_TOPK) for j in range(P_TOPK) if (i + 1) * (j + 1) <= P_TOPK]
    srow = lax.broadcasted_iota(jnp.int32, (SUBLANES, tm), 0)
    groups = []
    for g0 in range(0, len(cands), SUBLANES):
        blk = jnp.full((SUBLANES, tm), NEG_BIG, F32)
        for r, cv in enumerate(cands[g0:g0 + SUBLANES]):
            blk = jnp.where(srow == r, cv, blk)
        groups.append(blk)
    cand = jnp.concatenate(groups, axis=0)
    top = u[0] + v[0]
    z = jnp.zeros((1, tm), F32)
    tau = top
    for _ in range(P_TOPK):
        mx = jnp.max(cand, axis=0, keepdims=True)
        z = z + jnp.exp(mx - top)
        tau = mx
        cand = jnp.where(cand == mx, NEG_BIG, cand)
    cnt = jnp.zeros(s1.shape, F32)
    for j in range(P_TOPK):
        cnt = jnp.where(s1 + v[j] >= tau, float(j + 1), cnt)
    n1_ref[...] = cnt
    c1_ref[...] = jnp.exp(s1 - u[0]) / z
    r2_ref[...] = pltpu.bitcast(rank2.astype(BF16), jnp.uint32)
    e2_ref[...] = pltpu.bitcast(jnp.exp(s2 - v[0]).astype(BF16), jnp.uint32)


def _topk(st, tm=256):
    rows, n = st.shape
    tm = _tile(n, tm)
    blk = pl.BlockSpec((None, P_NKEYS, tm), lambda i, h: (h, 0, i))
    pblk = pl.BlockSpec((None, P_NKEYS // 2, tm), lambda i, h: (h, 0, i))
    return pl.pallas_call(
        functools.partial(_topk_kernel, tm=tm),
        grid=(n // tm, P_HEADS),
        in_specs=[pl.BlockSpec((2 * P_NKEYS, tm), lambda i, h: (h, i))],
        out_specs=[blk, blk, pblk, pblk],
        out_shape=[
            jax.ShapeDtypeStruct((P_HEADS, P_NKEYS, n), F32),
            jax.ShapeDtypeStruct((P_HEADS, P_NKEYS, n), F32),
            jax.ShapeDtypeStruct((P_HEADS, P_NKEYS // 2, n), jnp.uint32),
            jax.ShapeDtypeStruct((P_HEADS, P_NKEYS // 2, n), jnp.uint32),
        ],
        compiler_params=_cparams(("parallel", "arbitrary")),
        name="peer_topk",
    )(st)


PEER_TE = SUBLANES * P_NKEYS
BF16_ROWS = 2 * SUBLANES


def _peer_kernel(xn_ref, u_ref, vt_ref, n1_ref, c1_ref, r2_ref, e2_ref, y_ref, w_sc, *, tm):
    e = pl.program_id(1)

    @pl.when(e == 0)
    def _():
        y_ref[...] = jnp.zeros(y_ref.shape, F32)

    hid = jnp.dot(u_ref[...], xn_ref[...], preferred_element_type=F32)
    for al in range(PEER_TE // P_NKEYS):
        for lc in range(tm // LANES):
            cols = slice(lc * LANES, (lc + 1) * LANES)
            cnt, c1 = [], []
            for h in range(P_HEADS):
                cnt.append(jnp.broadcast_to(n1_ref[h, al:al + 1, cols], (BF16_ROWS, LANES)).astype(BF16))
                c1.append(jnp.broadcast_to(c1_ref[h, al:al + 1, cols], (BF16_ROWS, LANES)).astype(BF16))
            for b0 in range(0, P_NKEYS, BF16_ROWS):
                prow = slice(b0 // 2, (b0 + BF16_ROWS) // 2)
                acc = jnp.zeros((BF16_ROWS, LANES), BF16)
                for h in range(P_HEADS):
                    keep = pltpu.bitcast(r2_ref[h, prow, cols], BF16) < cnt[h]
                    gate = pltpu.bitcast(e2_ref[h, prow, cols], BF16) * c1[h]
                    acc = acc + jnp.where(keep, gate, jnp.zeros((), BF16))
                rows = slice(al * P_NKEYS + b0, al * P_NKEYS + b0 + BF16_ROWS)
                hv = hid[rows, cols]
                gelu = 0.5 * hv * (1.0 + lax.erf(hv * (2.0 ** -0.5)))
                w_sc[(al * P_NKEYS + b0) // 2:(al * P_NKEYS + b0 + BF16_ROWS) // 2, cols] = pltpu.bitcast(
                    acc * gelu.astype(BF16), jnp.uint32)
    y_ref[...] += jnp.dot(vt_ref[...], pltpu.bitcast(w_sc[...], BF16), preferred_element_type=F32)


def _peer_dense(xn_t, u_bf, vt_bf, n1, c1, r2, e2, tm=512):
    d, n = xn_t.shape
    nexp = u_bf.shape[0]
    te = PEER_TE
    tm = _tile(n, tm)
    assert nexp == P_NKEYS * P_NKEYS and tm % LANES == 0
    return pl.pallas_call(
        functools.partial(_peer_kernel, tm=tm),
        grid=(n // tm, nexp // te),
        in_specs=[
            pl.BlockSpec((d, tm), lambda i, e: (0, i)),
            pl.BlockSpec((te, d), lambda i, e: (e, 0)),
            pl.BlockSpec((d, te), lambda i, e: (0, e)),
            pl.BlockSpec((P_HEADS, SUBLANES, tm), lambda i, e: (0, e, i)),
            pl.BlockSpec((P_HEADS, SUBLANES, tm), lambda i, e: (0, e, i)),
            pl.BlockSpec((P_HEADS, P_NKEYS // 2, tm), lambda i, e: (0, 0, i)),
            pl.BlockSpec((P_HEADS, P_NKEYS // 2, tm), lambda i, e: (0, 0, i)),
        ],
        out_specs=pl.BlockSpec((d, tm), lambda i, e: (0, i)),
        out_shape=jax.ShapeDtypeStruct((d, n), F32),
        scratch_shapes=[pltpu.VMEM((te // 2, tm), jnp.uint32)],
        compiler_params=_cparams(("parallel", "arbitrary")),
        name="peer_dense",
    )(xn_t, u_bf, vt_bf, n1, c1, r2, e2)


def _final_kernel(x_ref, yt_ref, g_ref, o_ref):
    x = x_ref[...] + yt_ref[...].T
    ms = jnp.mean(x * x, axis=-1, keepdims=True)
    o_ref[...] = x * lax.rsqrt(ms + EPS) * g_ref[...]


def _final(x1, yt, g, tm=256):
    n, d = x1.shape
    tm = _tile(n, tm)
    return pl.pallas_call(
        _final_kernel,
        grid=(n // tm,),
        in_specs=[
            pl.BlockSpec((tm, d), lambda i: (i, 0)),
            pl.BlockSpec((d, tm), lambda i: (0, i)),
            pl.BlockSpec((1, d), lambda i: (0, 0)),
        ],
        out_specs=pl.BlockSpec((tm, d), lambda i: (i, 0)),
        out_shape=jax.ShapeDtypeStruct((n, d), F32),
        compiler_params=_cparams(("parallel",)),
        name="final_norm",
    )(x1, yt, g.reshape(1, d))


def _prepare_weights(w_in, w_a_proj, w_b_proj, w_o, w_pq, sub_keys, u_experts, v_experts):
    aw = A_HEADS * A_HEAD_DIM
    qk = 2 * B_HEADS * B_QK_DIM
    bw = B_HEADS * B_V_DIM
    w = w_in[0]
    c0 = 5 * aw
    return dict(
        w_nat=w[:, :c0].astype(BF16),
        w_qv_t=jnp.concatenate([w[:, c0:c0 + qk], w[:, c0 + 2 * qk:c0 + 2 * qk + bw]], axis=1).T.astype(BF16),
        w_k=w[:, c0 + qk:c0 + 2 * qk].astype(BF16),
        w_gate=w[:, c0 + 2 * qk + bw:].astype(BF16),
        w_a=w_a_proj[0].astype(BF16),
        w_b=w_b_proj[0].astype(BF16),
        w_o=w_o[0].astype(BF16),
        keyproj=_keyproj(sub_keys[0], w_pq[0]),
        u=u_experts[0].astype(BF16),
        vt=v_experts[0].T.astype(BF16),
    )


def _encoder(x, wts, norm1_g, lb_logits, a_norm_g, lam, b_norm_g, norm2_g, final_g):
    bsz, t, d = x.shape
    n = bsz * t
    x2 = x.reshape(n, d)

    xn = _rmsnorm(x2, norm1_g[0], BF16)
    z_nat = _matmul(xn, wts["w_nat"], F32, name="inproj_nat")
    gate = _matmul(xn, wts["w_gate"], F32, name="inproj_gate")
    qv_t = _matmul_nt_batched(wts["w_qv_t"], xn, bsz, t, BF16, name="inproj_qv_t")
    kaug = _kaug(xn, wts["w_k"], bsz, t)

    o_f, o_b = _hgrn(z_nat, lb_logits, bsz, t)
    on = _attention(qv_t, kaug, lam[0], b_norm_g[0], bsz, t)

    mixed = _merge(o_f, o_b, z_nat, a_norm_g[0], on, gate, wts["w_a"], wts["w_b"])
    x1 = _matmul(mixed, wts["w_o"], F32, residual=x2, name="out_proj")

    xn2_t = _rmsnorm_t(x1, norm2_g[0], BF16)
    st = _matmul(wts["keyproj"], xn2_t, F32, name="peer_scores")
    n1, c1, r2, e2 = _topk(st)
    yt = _peer_dense(xn2_t, wts["u"], wts["vt"], n1, c1, r2, e2)
    return _final(x1, yt, final_g).reshape(bsz, t, d)


def kernel(x_prompt, x_sample, norm1_g, w_in, lb_logits, a_norm_g, w_a_proj, lam, b_norm_g, w_b_proj, w_o, norm2_g, w_pq, sub_keys, u_experts, v_experts, final_g):
    assert w_in.shape[0] == 1
    wts = _prepare_weights(w_in, w_a_proj, w_b_proj, w_o, w_pq, sub_keys, u_experts, v_experts)
    args = (wts, norm1_g, lb_logits, a_norm_g, lam, b_norm_g, norm2_g, final_g)
    return (_encoder(x_prompt, *args), _encoder(x_sample, *args))
```

```python
import functools
import math

import jax
import jax.numpy as jnp
from jax import lax
from jax.experimental import pallas as pl
from jax.experimental.pallas import tpu as pltpu

F32 = jnp.float32
BF16 = jnp.bfloat16

A_HEADS = 8
A_HEAD_DIM = 128
B_HEADS = 8
B_QK_DIM = 64
B_V_DIM = 128
P_HEADS = 8
P_NKEYS = 128
P_HALF = 128
P_TOPK = 16
EPS = 1e-6
LAM_INIT = 0.8 - 0.6 * math.exp(-0.3 * 0)

LANES = 128
SUBLANES = 8
VMEM_LIMIT_BYTES = 56 * 1024 * 1024

NEG_BIG = -1e30
HGRN_CHUNK = 64
HGRN_SUB = SUBLANES
HGRN_HEADS = 8
KAUG_HEADS = 2


def _cparams(sem, flags=None):
    return pltpu.CompilerParams(dimension_semantics=sem, vmem_limit_bytes=VMEM_LIMIT_BYTES, flags=flags)


def _tile(n, want):
    t = min(n, want)
    assert n % t == 0, (n, want)
    return t


def _rmsnorm_kernel(x_ref, g_ref, o_ref):
    x = x_ref[...]
    ms = jnp.mean(x * x, axis=-1, keepdims=True)
    o_ref[...] = (x * lax.rsqrt(ms + EPS) * g_ref[...]).astype(o_ref.dtype)


def _rmsnorm(x, g, out_dtype, tm=512):
    n, d = x.shape
    tm = _tile(n, tm)
    return pl.pallas_call(
        _rmsnorm_kernel,
        grid=(n // tm,),
        in_specs=[pl.BlockSpec((tm, d), lambda i: (i, 0)), pl.BlockSpec((1, d), lambda i: (0, 0))],
        out_specs=pl.BlockSpec((tm, d), lambda i: (i, 0)),
        out_shape=jax.ShapeDtypeStruct((n, d), out_dtype),
        compiler_params=_cparams(("parallel",)),
        name="rmsnorm",
    )(x, g.reshape(1, d))


def _rmsnorm_t_kernel(x_ref, g_ref, o_ref):
    x = x_ref[...]
    ms = jnp.mean(x * x, axis=-1, keepdims=True)
    o_ref[...] = (x * lax.rsqrt(ms + EPS) * g_ref[...]).T.astype(o_ref.dtype)


def _rmsnorm_t(x, g, out_dtype, tm=512):
    n, d = x.shape
    tm = _tile(n, tm)
    return pl.pallas_call(
        _rmsnorm_t_kernel,
        grid=(n // tm,),
        in_specs=[pl.BlockSpec((tm, d), lambda i: (i, 0)), pl.BlockSpec((1, d), lambda i: (0, 0))],
        out_specs=pl.BlockSpec((d, tm), lambda i: (0, i)),
        out_shape=jax.ShapeDtypeStruct((d, n), out_dtype),
        compiler_params=_cparams(("parallel",)),
        name="rmsnorm_t",
    )(x, g.reshape(1, d))


def _mm_kernel(a_ref, w_ref, o_ref):
    o_ref[...] = jnp.dot(a_ref[...], w_ref[...], preferred_element_type=F32).astype(o_ref.dtype)


def _mm_sigmoid_kernel(a_ref, w_ref, o_ref):
    o_ref[...] = jax.nn.sigmoid(jnp.dot(a_ref[...], w_ref[...], preferred_element_type=F32)).astype(o_ref.dtype)


def _mm_res_kernel(a_ref, w_ref, r_ref, o_ref):
    o_ref[...] = (r_ref[...] + jnp.dot(a_ref[...], w_ref[...], preferred_element_type=F32)).astype(o_ref.dtype)


def _matmul(a, w, out_dtype, residual=None, sigmoid=False, tm=1024, tn=1024, name="matmul"):
    n, k = a.shape
    m = w.shape[1]
    tm, tn = _tile(n, tm), _tile(m, tn)
    in_specs = [pl.BlockSpec((tm, k), lambda i, j: (i, 0)), pl.BlockSpec((k, tn), lambda i, j: (0, j))]
    args = [a, w]
    assert not (sigmoid and residual is not None)
    kern = _mm_sigmoid_kernel if sigmoid else _mm_kernel
    if residual is not None:
        in_specs.append(pl.BlockSpec((tm, tn), lambda i, j: (i, j)))
        args.append(residual)
        kern = _mm_res_kernel
    return pl.pallas_call(
        kern,
        grid=(n // tm, m // tn),
        in_specs=in_specs,
        out_specs=pl.BlockSpec((tm, tn), lambda i, j: (i, j)),
        out_shape=jax.ShapeDtypeStruct((n, m), out_dtype),
        compiler_params=_cparams(("parallel", "arbitrary")),
        name=name,
    )(*args)


def _mm_nt_kernel(wt_ref, a_ref, o_ref):
    o_ref[...] = lax.dot_general(
        wt_ref[...], a_ref[...], (((1,), (1,)), ((), ())), preferred_element_type=F32
    ).astype(o_ref.dtype)


def _matmul_nt_batched(wt, a, bsz, t, out_dtype, tm=1024, tn=1024, name="matmul_nt"):
    m, k = wt.shape
    tm, tn = _tile(t, tm), _tile(m, tn)
    nt = t // tm
    return pl.pallas_call(
        _mm_nt_kernel,
        grid=(bsz, nt, m // tn),
        in_specs=[
            pl.BlockSpec((tn, k), lambda b, i, j: (j, 0)),
            pl.BlockSpec((tm, k), lambda b, i, j: (b * nt + i, 0)),
        ],
        out_specs=pl.BlockSpec((None, tn, tm), lambda b, i, j: (b, j, i)),
        out_shape=jax.ShapeDtypeStruct((bsz, m, t), out_dtype),
        compiler_params=_cparams(("parallel", "parallel", "arbitrary")),
        name=name,
    )(wt, a)


def _kaug_kernel(a_ref, w_ref, o_ref, *, tm):
    i = pl.program_id(1)
    hp = pl.program_id(2)
    k = jnp.dot(a_ref[...], w_ref[...], preferred_element_type=F32)
    pos = i * tm + lax.broadcasted_iota(jnp.int32, (tm, LANES), 0)
    lane = lax.broadcasted_iota(jnp.int32, (tm, LANES), 1)
    hi = (pos >> 6).astype(F32)
    lo = (pos & 63).astype(F32)
    for hh in range(KAUG_HEADS):
        slope = jnp.exp2(-(hp * KAUG_HEADS + hh + 1).astype(F32))
        aug = jnp.where(lane == 0, slope * 64.0 * hi,
                        jnp.where(lane == 1, slope * lo,
                                  jnp.where(lane == 2, -64.0 * slope, jnp.where(lane == 3, -slope, 0.0))))
        o_ref[hh] = jnp.concatenate([k[:, hh * LANES:(hh + 1) * LANES], aug], axis=1).astype(o_ref.dtype)


def _kaug(xn, wk, bsz, t, tm=1024):
    n, d = xn.shape
    assert t <= 64 * 256
    tm = _tile(t, tm)
    nt = t // tm
    return pl.pallas_call(
        functools.partial(_kaug_kernel, tm=tm),
        grid=(bsz, nt, B_HEADS // KAUG_HEADS),
        in_specs=[
            pl.BlockSpec((tm, d), lambda b, i, h: (b * nt + i, 0)),
            pl.BlockSpec((d, KAUG_HEADS * 2 * B_QK_DIM), lambda b, i, h: (0, h)),
        ],
        out_specs=pl.BlockSpec((None, KAUG_HEADS, tm, 2 * LANES), lambda b, i, h: (b, h, i, 0)),
        out_shape=jax.ShapeDtypeStruct((bsz, B_HEADS, t, 2 * LANES), BF16),
        compiler_params=_cparams(("parallel", "parallel", "arbitrary")),
        name="kaug",
    )(xn, wk)


def _attn_kernel(qt_ref, k_ref, vt_ref, lam_ref, g_ref, o_ref, qa_sc, m_sc, l_sc, acc_sc, *, tq, nk):
    h = pl.program_id(1)
    qi = pl.program_id(2)
    slope = jnp.exp2(-(h + 1).astype(F32))

    qt = qt_ref[...].astype(F32) * (B_QK_DIM ** -0.5)
    row = lax.broadcasted_iota(jnp.int32, (LANES, tq), 0)
    ipos = qi * tq + lax.broadcasted_iota(jnp.int32, (LANES, tq), 1)
    ihi = (ipos >> 6).astype(F32)
    ilo = (ipos & 63).astype(F32)
    aug = jnp.where(row < 2, 1.0, jnp.where(row == 2, ihi, jnp.where(row == 3, ilo, 0.0)))
    for c in range(2):
        qm = jnp.where((row >= B_QK_DIM * c) & (row < B_QK_DIM * (c + 1)), qt, 0.0)
        qa_sc[c, 0] = jnp.concatenate([qm, aug], axis=0).astype(BF16)
        qa_sc[c, 1] = jnp.concatenate([qm, -aug], axis=0).astype(BF16)

    m_sc[...] = jnp.full(m_sc.shape, NEG_BIG, F32)
    l_sc[...] = jnp.zeros(l_sc.shape, F32)
    acc_sc[...] = jnp.zeros(acc_sc.shape, F32)

    def update(c, s, vt):
        m_old = m_sc[c]
        m_new = jnp.maximum(m_old, jnp.max(s, axis=0, keepdims=True))
        alpha = jnp.exp(m_old - m_new)
        p = jnp.exp(s - m_new)
        l_sc[c] = alpha * l_sc[c] + jnp.sum(p, axis=0, keepdims=True)
        acc_sc[c] = alpha * acc_sc[c] + jnp.dot(vt, p.astype(BF16), preferred_element_type=F32)
        m_sc[c] = m_new

    def side_tile(jj, carry):
        j = jj + (jj >= qi).astype(jnp.int32)
        side = (j > qi).astype(jnp.int32)
        off = pl.multiple_of(j * tq, tq)
        kt = k_ref[pl.ds(off, tq), :]
        vt = vt_ref[:, pl.ds(off, tq)]
        scores = [jnp.dot(kt, qa_sc[c, side], preferred_element_type=F32) for c in range(2)]
        for c in range(2):
            update(c, scores[c], vt)
        return carry

    lax.fori_loop(0, nk - 1, side_tile, 0)

    off = pl.multiple_of(qi * tq, tq)
    kt = k_ref[pl.ds(off, tq), 0:LANES]
    vt = vt_ref[:, pl.ds(off, tq)]
    jj = lax.broadcasted_iota(jnp.int32, (tq, tq), 0)
    ii = lax.broadcasted_iota(jnp.int32, (tq, tq), 1)
    bias = -slope * jnp.abs(ii - jj).astype(F32)
    for c in range(2):
        s = jnp.dot(kt, qa_sc[c, 0, 0:LANES, :], preferred_element_type=F32) + bias
        update(c, s, vt)

    lf = lam_ref[...]
    lam_full = (jnp.exp(jnp.sum(lf[0:1] * lf[1:2], axis=-1, keepdims=True))
                - jnp.exp(jnp.sum(lf[2:3] * lf[3:4], axis=-1, keepdims=True)) + LAM_INIT)
    o = acc_sc[0] / l_sc[0] - lam_full * (acc_sc[1] / l_sc[1])
    ms = jnp.mean(o * o, axis=0, keepdims=True)
    on = o * lax.rsqrt(ms + EPS)
    on = on.T * g_ref[...] * (1.0 - LAM_INIT)
    o_ref[...] = on.astype(o_ref.dtype)


def _attention(qv_t, kaug, lam, b_norm_g, bsz, t, tq=1024):
    tq = _tile(t, tq)
    nq = t // tq
    kern = functools.partial(_attn_kernel, tq=tq, nk=nq)
    qv = qv_t.reshape(bsz * 2 * B_HEADS, LANES, t)
    return pl.pallas_call(
        kern,
        grid=(bsz, B_HEADS, nq),
        in_specs=[
            pl.BlockSpec((None, LANES, tq), lambda b, h, q: (b * 2 * B_HEADS + h, 0, q)),
            pl.BlockSpec((None, None, t, 2 * LANES), lambda b, h, q: (b, h, 0, 0)),
            pl.BlockSpec((None, LANES, t), lambda b, h, q: (b * 2 * B_HEADS + B_HEADS + h, 0, 0)),
            pl.BlockSpec((4, B_QK_DIM), lambda b, h, q: (0, 0)),
            pl.BlockSpec((1, B_V_DIM), lambda b, h, q: (0, h)),
        ],
        out_specs=pl.BlockSpec((tq, B_V_DIM), lambda b, h, q: (b * nq + q, h)),
        out_shape=jax.ShapeDtypeStruct((bsz * t, B_HEADS * B_V_DIM), BF16),
        scratch_shapes=[
            pltpu.VMEM((2, 2, 2 * LANES, tq), BF16),
            pltpu.VMEM((2, 1, tq), F32),
            pltpu.VMEM((2, 1, tq), F32),
            pltpu.VMEM((2, B_V_DIM, tq), F32),
        ],
        compiler_params=_cparams(("parallel", "parallel", "arbitrary")),
        name="diff_attention",
    )(qv, kaug, qv, lam, b_norm_g.reshape(1, -1))


def _hgrn_chunk(q, k, v, g, st_ref, rev):
    c, sub = HGRN_CHUNK, HGRN_SUB
    nb = c // sub
    r_i = lax.broadcasted_iota(jnp.int32, (c, c), 0)
    c_i = lax.broadcasted_iota(jnp.int32, (c, c), 1)
    tri = jnp.where((c_i >= r_i) if rev else (c_i <= r_i), 1.0, 0.0).astype(F32)
    b = jnp.dot(tri, g, preferred_element_type=F32, precision=lax.Precision.HIGHEST)
    btot = b[0:1] if rev else b[c - 1:c]
    st = st_ref[...]

    qe = q * jnp.exp(b)
    o = lax.dot_general(qe.astype(BF16), st.astype(BF16), (((1,), (1,)), ((), ())), preferred_element_type=F32)

    q_parts, k_parts = [], []
    for j in (range(1, nb) if rev else range(nb - 1)):
        if rev:
            ej = b[sub * j:sub * j + 1]
            qrows = slice(0, sub * j)
        else:
            ej = b[sub * j + sub - 1:sub * j + sub]
            qrows = slice(sub * (j + 1), c)
        qp = q[qrows] * jnp.exp(b[qrows] - ej)
        kp = k[sub * j:sub * (j + 1)] * jnp.exp(ej - b[sub * j:sub * (j + 1)])
        zq = jnp.zeros((c - qp.shape[0], LANES), F32)
        q_parts.append(jnp.concatenate([qp, zq] if rev else [zq, qp], axis=0))
        pieces = []
        if j > 0:
            pieces.append(jnp.zeros((sub * j, LANES), F32))
        pieces.append(kp)
        if j < nb - 1:
            pieces.append(jnp.zeros((c - sub * (j + 1), LANES), F32))
        k_parts.append(jnp.concatenate(pieces, axis=0))
    qcat = jnp.concatenate(q_parts, axis=1).astype(BF16)
    kcat = jnp.concatenate(k_parts, axis=1).astype(BF16)
    sc = lax.dot_general(qcat, kcat, (((1,), (1,)), ((), ())), preferred_element_type=F32)
    o = o + jnp.dot(sc.astype(BF16), v.astype(BF16), preferred_element_type=F32)

    b3, q3, k3, v3 = (a.reshape(nb, sub, LANES) for a in (b, q, k, v))
    srow = lax.broadcasted_iota(jnp.int32, (1, sub, 1), 1)
    acc = jnp.zeros((nb, sub, LANES), F32)
    for j in range(sub):
        keep = (srow <= j) if rev else (srow >= j)
        w = jnp.exp(jnp.where(keep, b3 - b3[:, j:j + 1, :], NEG_BIG))
        sj = jnp.sum(q3 * w * k3[:, j:j + 1, :], axis=-1, keepdims=True)
        acc = acc + sj * v3[:, j:j + 1, :]
    o = o + acc.reshape(c, LANES)

    kd = k * jnp.exp(btot - b)
    st_ref[...] = st * jnp.exp(btot) + jnp.dot(v.T.astype(BF16), kd.astype(BF16), preferred_element_type=F32)
    return o


def _hgrn_kernel(qf_ref, if_ref, zf_ref, qb_ref, ib_ref, zb_ref, lb_ref, of_ref, ob_ref, st_sc, *, ts):
    @pl.when(pl.program_id(2) == 0)
    def _():
        st_sc[...] = jnp.zeros(st_sc.shape, F32)

    lg = lb_ref[...]
    lmax = jnp.max(lg, axis=1, keepdims=True)
    le = jnp.exp(lg - lmax)
    lb = le[:, 0, :] / jnp.sum(le, axis=1)

    nc = ts // HGRN_CHUNK

    def gates(qa, ia, z, lbd):
        f = lbd + (1.0 - lbd) * jax.nn.sigmoid(z)
        return qa * jax.nn.sigmoid(qa), 1.0 - f, ia, jnp.log(f)

    def body(ci, carry):
        off = pl.multiple_of(ci * HGRN_CHUNK, HGRN_CHUNK)
        rows = pl.ds(off, HGRN_CHUNK)
        roff = pl.multiple_of((nc - 1 - ci) * HGRN_CHUNK, HGRN_CHUNK)
        rrows = pl.ds(roff, HGRN_CHUNK)
        for hh in range(HGRN_HEADS):
            cols = slice(hh * A_HEAD_DIM, (hh + 1) * A_HEAD_DIM)
            q, k, v, g = gates(qf_ref[rows, cols], if_ref[rows, cols], zf_ref[rows, cols], lb[0:1, cols])
            of_ref[rows, cols] = _hgrn_chunk(q, k, v, g, st_sc.at[2 * hh], rev=False)
            q, k, v, g = gates(qb_ref[rrows, cols], ib_ref[rrows, cols], zb_ref[rrows, cols], lb[1:2, cols])
            ob_ref[rrows, cols] = _hgrn_chunk(q, k, v, g, st_sc.at[2 * hh + 1], rev=True)
        return carry

    lax.fori_loop(0, nc, body, 0)


def _hgrn(z_nat, lb_logits, bsz, t, ts=512):
    assert lb_logits.shape[1] == 2
    n = bsz * t
    ts = _tile(t, ts)
    ns = t // ts
    hw = HGRN_HEADS * A_HEAD_DIM
    ng = A_HEADS // HGRN_HEADS

    def fwd(col):
        return pl.BlockSpec((ts, hw), lambda b, h, i: (b * ns + i, col * ng + h))

    def bwd(col):
        return pl.BlockSpec((ts, hw), lambda b, h, i: (b * ns + ns - 1 - i, col * ng + h))

    return pl.pallas_call(
        functools.partial(_hgrn_kernel, ts=ts),
        grid=(bsz, ng, ns),
        in_specs=[fwd(0), fwd(1), fwd(2), bwd(0), bwd(1), bwd(3),
                  pl.BlockSpec((2, 2, hw), lambda b, h, i: (0, 0, h))],
        out_specs=[
            pl.BlockSpec((ts, hw), lambda b, h, i: (b * ns + i, h)),
            pl.BlockSpec((ts, hw), lambda b, h, i: (b * ns + ns - 1 - i, h)),
        ],
        out_shape=[jax.ShapeDtypeStruct((n, A_HEADS * A_HEAD_DIM), F32)] * 2,
        scratch_shapes=[pltpu.VMEM((2 * HGRN_HEADS, A_HEAD_DIM, A_HEAD_DIM), F32)],
        compiler_params=_cparams(("parallel", "parallel", "arbitrary")),
        name="hgrn2",
    )(z_nat, z_nat, z_nat, z_nat, z_nat, z_nat, lb_logits)


def _merge_kernel(of_ref, ob_ref, ga_ref, ag_ref, on_ref, gate_ref, wa_ref, wb_ref, o_ref, *, d):
    o = of_ref[...] + ob_ref[...]
    parts = []
    for h in range(A_HEADS):
        oh = o[:, h * A_HEAD_DIM:(h + 1) * A_HEAD_DIM]
        ms = jnp.mean(oh * oh, axis=-1, keepdims=True)
        parts.append(oh * lax.rsqrt(ms + EPS))
    ga = ga_ref[...]
    oa = jnp.concatenate(parts, axis=1) * ag_ref[...] * (ga * jax.nn.sigmoid(ga))
    ya = jnp.dot(oa.astype(BF16), wa_ref[...], preferred_element_type=F32)
    yb = jnp.dot(on_ref[...], wb_ref[...], preferred_element_type=F32)
    gts = gate_ref[...].astype(F32)
    o_ref[...] = (gts[:, :d] * ya + gts[:, d:] * yb).astype(o_ref.dtype)


def _merge(o_f, o_b, z_nat, a_norm_g, on, gate, w_a, w_b, tm=512):
    n, aw = o_f.shape
    d = w_a.shape[1]
    tm = _tile(n, tm)
    ga_col = 4 * A_HEADS * A_HEAD_DIM // aw
    return pl.pallas_call(
        functools.partial(_merge_kernel, d=d),
        grid=(n // tm,),
        in_specs=[
            pl.BlockSpec((tm, aw), lambda i: (i, 0)),
            pl.BlockSpec((tm, aw), lambda i: (i, 0)),
            pl.BlockSpec((tm, aw), lambda i: (i, ga_col)),
            pl.BlockSpec((1, aw), lambda i: (0, 0)),
            pl.BlockSpec((tm, on.shape[1]), lambda i: (i, 0)),
            pl.BlockSpec((tm, 2 * d), lambda i: (i, 0)),
            pl.BlockSpec(w_a.shape, lambda i: (0, 0)),
            pl.BlockSpec(w_b.shape, lambda i: (0, 0)),
        ],
        out_specs=pl.BlockSpec((tm, d), lambda i: (i, 0)),
        out_shape=jax.ShapeDtypeStruct((n, d), BF16),
        compiler_params=_cparams(("parallel",)),
        name="merge",
    )(o_f, o_b, z_nat, a_norm_g.reshape(1, aw), on, gate, w_a, w_b)


def _keyproj_kernel(keys_ref, wpq_ref, o_ref):
    o_ref[...] = lax.dot_general(
        keys_ref[...], wpq_ref[...], (((1,), (1,)), ((), ())),
        preferred_element_type=F32, precision=lax.Precision.HIGHEST,
    ).astype(o_ref.dtype)


def _keyproj(sub_keys, w_pq):
    d = w_pq.shape[0]
    nhc = 2 * P_HEADS
    keys = sub_keys.reshape(nhc, P_NKEYS, P_HALF)
    return pl.pallas_call(
        _keyproj_kernel,
        grid=(nhc,),
        in_specs=[
            pl.BlockSpec((None, P_NKEYS, P_HALF), lambda i: (i, 0, 0)),
            pl.BlockSpec((d, P_HALF), lambda i: (0, i)),
        ],
        out_specs=pl.BlockSpec((P_NKEYS, d), lambda i: (i, 0)),
        out_shape=jax.ShapeDtypeStruct((nhc * P_NKEYS, d), BF16),
        compiler_params=_cparams(("parallel",)),
        name="peer_keyproj",
    )(keys, w_pq)


def _topk_kernel(s_ref, n1_ref, c1_ref, r2_ref, e2_ref, *, tm):
    s1 = s_ref[0:P_NKEYS, :]
    s2 = s_ref[P_NKEYS:2 * P_NKEYS, :]

    def top_sorted(s, with_rank):
        vals = []
        work = s
        rank = jnp.full(s.shape, float(P_TOPK), F32)
        for r in range(P_TOPK):
            mx = jnp.max(work, axis=0, keepdims=True)
            vals.append(mx)
            hit = work == mx
            if with_rank:
                rank = jnp.where(hit, float(r), rank)
            work = jnp.where(hit, NEG_BIG, work)
        return vals, rank

    u, _ = top_sorted(s1, False)
    v, rank2 = top_sorted(s2, True)
    cands = [u[i] + v[j] for i in range(P_TOPK) for j in range(P_TOPK) if (i + 1) * (j + 1) <= P_TOPK]
    srow = lax.broadcasted_iota(jnp.int32, (SUBLANES, tm), 0)
    groups = []
    for g0 in range(0, len(cands), SUBLANES):
        blk = jnp.full((SUBLANES, tm), NEG_BIG, F32)
        for r, cv in enumerate(cands[g0:g0 + SUBLANES]):
            blk = jnp.where(srow == r, cv, blk)
        groups.append(blk)
    cand = jnp.concatenate(groups, axis=0)
    top = u[0] + v[0]
    z = jnp.zeros((1, tm), F32)
    tau = top
    for _ in range(P_TOPK):
        mx = jnp.max(cand, axis=0, keepdims=True)
        z = z + jnp.exp(mx - top)
        tau = mx
        cand = jnp.where(cand == mx, NEG_BIG, cand)
    cnt = jnp.zeros(s1.shape, F32)
    for j in range(P_TOPK):
        cnt = jnp.where(s1 + v[j] >= tau, float(j + 1), cnt)
    n1_ref[...] = cnt
    c1_ref[...] = jnp.exp(s1 - u[0]) / z
    r2_ref[...] = pltpu.bitcast(rank2.astype(BF16), jnp.uint32)
    e2_ref[...] = pltpu.bitcast(jnp.exp(s2 - v[0]).astype(BF16), jnp.uint32)


def _topk(st, tm=256):
    rows, n = st.shape
    tm = _tile(n, tm)
    blk = pl.BlockSpec((None, P_NKEYS, tm), lambda i, h: (h, 0, i))
    pblk = pl.BlockSpec((None, P_NKEYS // 2, tm), lambda i, h: (h, 0, i))
    return pl.pallas_call(
        functools.partial(_topk_kernel, tm=tm),
        grid=(n // tm, P_HEADS),
        in_specs=[pl.BlockSpec((2 * P_NKEYS, tm), lambda i, h: (h, i))],
        out_specs=[blk, blk, pblk, pblk],
        out_shape=[
            jax.ShapeDtypeStruct((P_HEADS, P_NKEYS, n), F32),
            jax.ShapeDtypeStruct((P_HEADS, P_NKEYS, n), F32),
            jax.ShapeDtypeStruct((P_HEADS, P_NKEYS // 2, n), jnp.uint32),
            jax.ShapeDtypeStruct((P_HEADS, P_NKEYS // 2, n), jnp.uint32),
        ],
        compiler_params=_cparams(("parallel", "arbitrary")),
        name="peer_topk",
    )(st)


PEER_TE = SUBLANES * P_NKEYS
BF16_ROWS = 2 * SUBLANES


def _peer_kernel(xn_ref, u_ref, vt_ref, n1_ref, c1_ref, r2_ref, e2_ref, y_ref, w_sc, *, tm):
    e = pl.program_id(1)

    @pl.when(e == 0)
    def _():
        y_ref[...] = jnp.zeros(y_ref.shape, F32)

    hid = jnp.dot(u_ref[...], xn_ref[...], preferred_element_type=F32)
    for al in range(PEER_TE // P_NKEYS):
        for lc in range(tm // LANES):
            cols = slice(lc * LANES, (lc + 1) * LANES)
            cnt, c1 = [], []
            for h in range(P_HEADS):
                cnt.append(jnp.broadcast_to(n1_ref[h, al:al + 1, cols], (BF16_ROWS, LANES)).astype(BF16))
                c1.append(jnp.broadcast_to(c1_ref[h, al:al + 1, cols], (BF16_ROWS, LANES)).astype(BF16))
            for b0 in range(0, P_NKEYS, BF16_ROWS):
                prow = slice(b0 // 2, (b0 + BF16_ROWS) // 2)
                acc = jnp.zeros((BF16_ROWS, LANES), BF16)
                for h in range(P_HEADS):
                    keep = pltpu.bitcast(r2_ref[h, prow, cols], BF16) < cnt[h]
                    gate = pltpu.bitcast(e2_ref[h, prow, cols], BF16) * c1[h]
                    acc = acc + jnp.where(keep, gate, jnp.zeros((), BF16))
                rows = slice(al * P_NKEYS + b0, al * P_NKEYS + b0 + BF16_ROWS)
                hv = hid[rows, cols]
                gelu = 0.5 * hv * (1.0 + lax.erf(hv * (2.0 ** -0.5)))
                w_sc[(al * P_NKEYS + b0) // 2:(al * P_NKEYS + b0 + BF16_ROWS) // 2, cols] = pltpu.bitcast(
                    acc * gelu.astype(BF16), jnp.uint32)
    y_ref[...] += jnp.dot(vt_ref[...], pltpu.bitcast(w_sc[...], BF16), preferred_element_type=F32)


def _peer_dense(xn_t, u_bf, vt_bf, n1, c1, r2, e2, tm=512):
    d, n = xn_t.shape
    nexp = u_bf.shape[0]
    te = PEER_TE
    tm = _tile(n, tm)
    assert nexp == P_NKEYS * P_NKEYS and tm % LANES == 0
    return pl.pallas_call(
        functools.partial(_peer_kernel, tm=tm),
        grid=(n // tm, nexp // te),
        in_specs=[
            pl.BlockSpec((d, tm), lambda i, e: (0, i)),
            pl.BlockSpec((te, d), lambda i, e: (e, 0)),
            pl.BlockSpec((d, te), lambda i, e: (0, e)),
            pl.BlockSpec((P_HEADS, SUBLANES, tm), lambda i, e: (0, e, i)),
            pl.BlockSpec((P_HEADS, SUBLANES, tm), lambda i, e: (0, e, i)),
            pl.BlockSpec((P_HEADS, P_NKEYS // 2, tm), lambda i, e: (0, 0, i)),
            pl.BlockSpec((P_HEADS, P_NKEYS // 2, tm), lambda i, e: (0, 0, i)),
        ],
        out_specs=pl.BlockSpec((d, tm), lambda i, e: (0, i)),
        out_shape=jax.ShapeDtypeStruct((d, n), F32),
        scratch_shapes=[pltpu.VMEM((te // 2, tm), jnp.uint32)],
        compiler_params=_cparams(("parallel", "arbitrary")),
        name="peer_dense",
    )(xn_t, u_bf, vt_bf, n1, c1, r2, e2)


def _final_kernel(x_ref, yt_ref, g_ref, o_ref):
    x = x_ref[...] + yt_ref[...].T
    ms = jnp.mean(x * x, axis=-1, keepdims=True)
    o_ref[...] = x * lax.rsqrt(ms + EPS) * g_ref[...]


def _final(x1, yt, g, tm=256):
    n, d = x1.shape
    tm = _tile(n, tm)
    return pl.pallas_call(
        _final_kernel,
        grid=(n // tm,),
        in_specs=[
            pl.BlockSpec((tm, d), lambda i: (i, 0)),
            pl.BlockSpec((d, tm), lambda i: (0, i)),
            pl.BlockSpec((1, d), lambda i: (0, 0)),
        ],
        out_specs=pl.BlockSpec((tm, d), lambda i: (i, 0)),
        out_shape=jax.ShapeDtypeStruct((n, d), F32),
        compiler_params=_cparams(("parallel",)),
        name="final_norm",
    )(x1, yt, g.reshape(1, d))


def _prepare_weights(w_in, w_a_proj, w_b_proj, w_o, w_pq, sub_keys, u_experts, v_experts):
    aw = A_HEADS * A_HEAD_DIM
    qk = 2 * B_HEADS * B_QK_DIM
    bw = B_HEADS * B_V_DIM
    w = w_in[0]
    c0 = 5 * aw
    return dict(
        w_nat=w[:, :c0].astype(BF16),
        w_qv_t=jnp.concatenate([w[:, c0:c0 + qk], w[:, c0 + 2 * qk:c0 + 2 * qk + bw]], axis=1).T.astype(BF16),
        w_k=w[:, c0 + qk:c0 + 2 * qk].astype(BF16),
        w_gate=w[:, c0 + 2 * qk + bw:].astype(BF16),
        w_a=w_a_proj[0].astype(BF16),
        w_b=w_b_proj[0].astype(BF16),
        w_o=w_o[0].astype(BF16),
        keyproj=_keyproj(sub_keys[0], w_pq[0]),
        u=u_experts[0].astype(BF16),
        vt=v_experts[0].T.astype(BF16),
    )


def _encoder(x, wts, norm1_g, lb_logits, a_norm_g, lam, b_norm_g, norm2_g, final_g):
    bsz, t, d = x.shape
    n = bsz * t
    x2 = x.reshape(n, d)

    xn = _rmsnorm(x2, norm1_g[0], BF16)
    z_nat = _matmul(xn, wts["w_nat"], F32, name="inproj_nat")
    gate = _matmul(xn, wts["w_gate"], BF16, sigmoid=True, name="inproj_gate")
    qv_t = _matmul_nt_batched(wts["w_qv_t"], xn, bsz, t, BF16, name="inproj_qv_t")
    kaug = _kaug(xn, wts["w_k"], bsz, t)

    o_f, o_b = _hgrn(z_nat, lb_logits, bsz, t)
    on = _attention(qv_t, kaug, lam[0], b_norm_g[0], bsz, t)

    mixed = _merge(o_f, o_b, z_nat, a_norm_g[0], on, gate, wts["w_a"], wts["w_b"])
    x1 = _matmul(mixed, wts["w_o"], F32, residual=x2, name="out_proj")

    xn2_t = _rmsnorm_t(x1, norm2_g[0], BF16)
    st = _matmul(wts["keyproj"], xn2_t, F32, name="peer_scores")
    n1, c1, r2, e2 = _topk(st)
    yt = _peer_dense(xn2_t, wts["u"], wts["vt"], n1, c1, r2, e2)
    return _final(x1, yt, final_g).reshape(bsz, t, d)


def kernel(x_prompt, x_sample, norm1_g, w_in, lb_logits, a_norm_g, w_a_proj, lam, b_norm_g, w_b_proj, w_o, norm2_g, w_pq, sub_keys, u_experts, v_experts, final_g):
    assert w_in.shape[0] == 1
    wts = _prepare_weights(w_in, w_a_proj, w_b_proj, w_o, w_pq, sub_keys, u_experts, v_experts)
    args = (wts, norm1_g, lb_logits, a_norm_g, lam, b_norm_g, norm2_g, final_g)
    return (_encoder(x_prompt, *args), _encoder(x_sample, *args))
```
